```python
import jax, jax.numpy as jnp
from jax import lax
import numpy as np

D_MODEL = 4096
BATCH = 32
SEQ = 256
DEPTH = 2
DEC_BATCH = 4
DEC_SEQ = 2048
PAST_LEN = 256

GRID_W = 64
W_A = D_MODEL // 2
HEAD_DIM_A = 128
N_HEADS_A = W_A // HEAD_DIM_A
W_B = D_MODEL // 4
W_C = D_MODEL - W_A - W_B
HEAD_DIM_C = 256
N_HEADS_C = W_C // HEAD_DIM_C
WIN_H = 8
WIN_W = 16
CONV_K = 31
CHUNK = 128
ROPE_BASE = 10000.0
EPS = 1e-6
NEG_INF = -1e30
N_IN = 4 * W_A + 3 * W_B + 4 * W_C

kernel_name = "hymba_na_conformer_retention_dit_step"

F32 = jnp.float32


def rms_norm(x, g):
    xf = x.astype(F32)
    y = xf * lax.rsqrt(jnp.mean(xf * xf, -1, keepdims=True) + EPS)
    return (y * g.astype(F32)).astype(x.dtype)


def layer_norm(x, g, b):
    xf = x.astype(F32)
    mu = jnp.mean(xf, -1, keepdims=True)
    xc = xf - mu
    var = jnp.mean(xc * xc, -1, keepdims=True)
    return (xc * lax.rsqrt(var + EPS) * g.astype(F32) + b.astype(F32)).astype(x.dtype)


def modulation(cond, w_mod, b_mod):
    m = jax.nn.silu(cond) @ w_mod + b_mod
    return jnp.split(m, 3, axis=-1)


def split_cols(z):
    sizes = [W_A] * 4 + [W_B] * 3 + [W_C] * 4
    idx = np.cumsum(sizes)[:-1].tolist()
    return jnp.split(z, idx, axis=-1)


def rope_angles(pos, n_dims):
    half = n_dims // 2
    freqs = ROPE_BASE ** (-jnp.arange(half, dtype=F32) / half)
    return pos.astype(F32)[:, None] * freqs[None, :]


def rotate(x, ang):
    xf = x.astype(F32)
    x1, x2 = jnp.split(xf, 2, axis=-1)
    cos = jnp.cos(ang)[None, :, None, :]
    sin = jnp.sin(ang)[None, :, None, :]
    return jnp.concatenate([x1 * cos - x2 * sin, x1 * sin + x2 * cos], -1).astype(x.dtype)


def axial_rope(x):
    L = x.shape[1]
    t = jnp.arange(L)
    half = x.shape[-1] // 2
    xr, xc = jnp.split(x, 2, axis=-1)
    return jnp.concatenate([rotate(xr, rope_angles(t // GRID_W, half)),
                            rotate(xc, rope_angles(t % GRID_W, half))], -1)


def dense_attention(q, k, v):
    B, Lq, H, Dh = q.shape
    nb = Lq // CHUNK
    scale = Dh ** -0.5
    qb = q.reshape(B, nb, CHUNK, H, Dh).swapaxes(0, 1)

    def block(qi):
        s = jnp.einsum('bqhd,bkhd->bhqk', qi, k, preferred_element_type=F32) * scale
        p = jax.nn.softmax(s, axis=-1).astype(v.dtype)
        return jnp.einsum('bhqk,bkhd->bqhd', p, v)

    o = lax.map(block, qb)
    return o.swapaxes(0, 1).reshape(B, Lq, H * Dh)


def neighbourhood_attention(q, k, v, k_ctx, v_ctx, rpb):
    B, L, H, Dh = q.shape
    rows = L // GRID_W
    kh = min(WIN_H, rows)
    r = jnp.arange(rows)
    r0 = jnp.clip(r - kh // 2, 0, rows - kh)
    key_rows = r0[:, None] + jnp.arange(kh)[None, :]
    col = jnp.arange(GRID_W)
    c0 = jnp.clip(col - WIN_W // 2, 0, GRID_W - WIN_W)
    in_win = (col[None, :] >= c0[:, None]) & (col[None, :] < c0[:, None] + WIN_W)
    dr = key_rows - r[:, None] + (WIN_H - 1)
    dc = jnp.clip(col[None, :] - col[:, None], 1 - WIN_W, WIN_W - 1) + (WIN_W - 1)
    bias = rpb.astype(F32)[:, dr[:, None, :, None], dc[None, :, None, :]]
    bias = jnp.where(in_win[None, None, :, None, :], bias, NEG_INF)
    bias = bias.reshape(H, rows, GRID_W, kh * GRID_W)
    qg = q.reshape(B, rows, GRID_W, H, Dh)
    kg = k.reshape(B, rows, GRID_W, H, Dh)[:, key_rows].reshape(B, rows, kh * GRID_W, H, Dh)
    vg = v.reshape(B, rows, GRID_W, H, Dh)[:, key_rows].reshape(B, rows, kh * GRID_W, H, Dh)
    scale = Dh ** -0.5
    s_loc = jnp.einsum('brqhd,brkhd->bhrqk', qg, kg, preferred_element_type=F32) * scale + bias[None]
    s_ctx = jnp.einsum('brqhd,bkhd->bhrqk', qg, k_ctx, preferred_element_type=F32) * scale
    p = jax.nn.softmax(jnp.concatenate([s_loc, s_ctx], -1), axis=-1).astype(v.dtype)
    n_loc = kh * GRID_W
    o = (jnp.einsum('bhrqk,brkhd->brqhd', p[..., :n_loc], vg)
         + jnp.einsum('bhrqk,bkhd->brqhd', p[..., n_loc:], v_ctx))
    return o.reshape(B, L, H * Dh)


def conformer_conv(u, glu_gate, conv_w, conv_b, ln_g, ln_b, w_pw):
    h = u * jax.nn.sigmoid(glu_gate)
    h = lax.conv_general_dilated(h, conv_w[:, None, :], window_strides=(1,),
                                 padding=[(CONV_K // 2, CONV_K // 2)],
                                 dimension_numbers=('NWC', 'WIO', 'NWC'),
                                 feature_group_count=W_B) + conv_b
    h = jax.nn.silu(layer_norm(h, ln_g, ln_b))
    return h @ w_pw


def retention_chunked(q, k, v, log_gamma, s0):
    B, L, H, Dh = q.shape
    nc = L // CHUNK
    idx = jnp.arange(CHUNK, dtype=F32)
    diff = idx[:, None] - idx[None, :]
    lg = log_gamma[:, None, None]
    d_in = jnp.where(diff >= 0, jnp.exp(lg * jnp.maximum(diff, 0.0)), 0.0)
    xi = jnp.exp(log_gamma[:, None] * (idx + 1.0)).T[None, :, :, None]
    zeta = jnp.exp(log_gamma[:, None] * (CHUNK - 1.0 - idx)).T[None, :, :, None]
    g_chunk = jnp.exp(log_gamma * CHUNK)[None, :, None, None]

    def to_chunks(a):
        return a.astype(F32).reshape(B, nc, CHUNK, H, Dh).swapaxes(0, 1)

    def step(s, inp):
        qi, ki, vi = inp
        qk = jnp.einsum('bnhd,bmhd->bhnm', qi, ki) * d_in[None]
        inner = jnp.einsum('bhnm,bmhe->bnhe', qk, vi)
        cross = jnp.einsum('bnhd,bhde->bnhe', qi, s) * xi
        s_new = g_chunk * s + jnp.einsum('bmhd,bmhe->bhde', ki * zeta, vi)
        return s_new, inner + cross

    s_fin, o = lax.scan(step, s0.astype(F32), (to_chunks(q), to_chunks(k), to_chunks(v)))
    return o.swapaxes(0, 1).reshape(B, L, H, Dh), s_fin


def bi_retention(q, k, v, decay_logit, dir_scale, s0_f, s0_b):
    log_gamma = jax.nn.log_sigmoid(decay_logit.astype(F32))
    o_f, s_f = retention_chunked(q, k, v, log_gamma[0], s0_f)
    o_b, s_b = retention_chunked(q[:, ::-1], k[:, ::-1], v[:, ::-1], log_gamma[1], s0_b)
    ds = dir_scale.astype(F32)
    o = o_f * ds[0][None, None, :, None] + o_b[:, ::-1] * ds[1][None, None, :, None]
    return o, s_f, s_b


def mixer_layer(x, shift, scale, gate, lp, ctx=None):
    B, L, _ = x.shape
    h = (rms_norm(x, lp["norm_g"]) * (1.0 + scale) + shift).astype(x.dtype)
    z = h @ lp["w_in"]
    qa, ka, va, ga, ub, glub, gb, qc, kc, vc, gc = split_cols(z)
    qa = rms_norm(qa.reshape(B, L, N_HEADS_A, HEAD_DIM_A), lp["qn_g"])
    ka = rms_norm(ka.reshape(B, L, N_HEADS_A, HEAD_DIM_A), lp["kn_g"])
    va = va.reshape(B, L, N_HEADS_A, HEAD_DIM_A)
    qc = qc.reshape(B, L, N_HEADS_C, HEAD_DIM_C)
    kc = kc.reshape(B, L, N_HEADS_C, HEAD_DIM_C) * (HEAD_DIM_C ** -0.5)
    vc = vc.reshape(B, L, N_HEADS_C, HEAD_DIM_C)
    if ctx is None:
        oa = dense_attention(qa, ka, va)
        zero = jnp.zeros((B, N_HEADS_C, HEAD_DIM_C, HEAD_DIM_C), F32)
        oc, s_f, s_b = bi_retention(qc, kc, vc, lp["ret_decay_logit"], lp["ret_dir_scale"], zero, zero)
        new = (ka, va, jnp.stack([s_f, s_b], axis=1))
    else:
        k_ctx, v_ctx, s_ctx = ctx
        oa = neighbourhood_attention(axial_rope(qa), axial_rope(ka), va, k_ctx, v_ctx, lp["rpb"])
        ang_t = rope_angles(jnp.arange(L), HEAD_DIM_C)
        oc, _, _ = bi_retention(rotate(qc, ang_t), rotate(kc, ang_t), vc, lp["ret_decay_logit"],
                                lp["ret_dir_scale"], s_ctx[:, 0], s_ctx[:, 1])
        new = None
    ob = conformer_conv(ub, glub, lp["conv_w"], lp["conv_b"], lp["cln_g"], lp["cln_b"], lp["w_pw"])
    oc = rms_norm(oc, lp["ret_gn_g"].reshape(N_HEADS_C, HEAD_DIM_C)).astype(x.dtype).reshape(B, L, W_C)
    o = jnp.concatenate([oa * jax.nn.silu(ga), ob * jax.nn.silu(gb), oc * jax.nn.silu(gc)], -1)
    return x + gate * (o @ lp["w_out"]), new


def setup_inputs(seed: int = 0) -> dict:
    key = jax.random.key(seed)
    ks = jax.random.split(key, 24)

    def nrm(k, shape, s):
        return s * jax.random.normal(k, shape, F32)

    base = jnp.log(2.0 ** (5.0 + jnp.arange(N_HEADS_C, dtype=F32)) - 1.0)
    return {
        "x_prompt": nrm(ks[0], (BATCH, SEQ, D_MODEL), 1.0),
        "x_sample": nrm(ks[1], (DEC_BATCH, DEC_SEQ, D_MODEL), 1.0),
        "cache_k": nrm(ks[2], (DEC_BATCH, DEPTH, PAST_LEN, N_HEADS_A, HEAD_DIM_A), 1.0),
        "cache_v": nrm(ks[3], (DEC_BATCH, DEPTH, PAST_LEN, N_HEADS_A, HEAD_DIM_A), 1.0),
        "state_ret": nrm(ks[4], (DEC_BATCH, DEPTH, 2, N_HEADS_C, HEAD_DIM_C, HEAD_DIM_C), 0.1),
        "c": nrm(ks[5], (DEC_BATCH, D_MODEL), 1.0),
        "c_ctx": nrm(ks[6], (D_MODEL,), 1.0),
        "norm_g": 1.0 + nrm(ks[7], (DEPTH, D_MODEL), 0.02),
        "w_mod": nrm(ks[8], (DEPTH, D_MODEL, 3 * D_MODEL), 0.5 * D_MODEL ** -0.5),
        "b_mod": nrm(ks[9], (DEPTH, 3 * D_MODEL), 0.02),
        "w_in": nrm(ks[10], (DEPTH, D_MODEL, N_IN), D_MODEL ** -0.5),
        "qn_g": 1.0 + nrm(ks[11], (DEPTH, HEAD_DIM_A), 0.02),
        "kn_g": 1.0 + nrm(ks[12], (DEPTH, HEAD_DIM_A), 0.02),
        "rpb": nrm(ks[13], (DEPTH, N_HEADS_A, 2 * WIN_H - 1, 2 * WIN_W - 1), 0.1),
        "conv_w": nrm(ks[14], (DEPTH, CONV_K, W_B), CONV_K ** -0.5),
        "conv_b": nrm(ks[15], (DEPTH, W_B), 0.02),
        "cln_g": 1.0 + nrm(ks[16], (DEPTH, W_B), 0.02),
        "cln_b": nrm(ks[17], (DEPTH, W_B), 0.02),
        "w_pw": nrm(ks[18], (DEPTH, W_B, W_B), W_B ** -0.5),
        "ret_decay_logit": base[None, None, :] + nrm(ks[19], (DEPTH, 2, N_HEADS_C), 0.1),
        "ret_dir_scale": 1.0 + nrm(ks[20], (DEPTH, 2, N_HEADS_C), 0.1),
        "ret_gn_g": 1.0 + nrm(ks[21], (DEPTH, W_C), 0.02),
        "w_out": nrm(ks[22], (DEPTH, D_MODEL, D_MODEL), D_MODEL ** -0.5),
    }


def reference(x_prompt, x_sample, cache_k, cache_v, state_ret, c, c_ctx, norm_g, w_mod, b_mod, w_in,
              qn_g, kn_g, rpb, conv_w, conv_b, cln_g, cln_b, w_pw, ret_decay_logit, ret_dir_scale,
              ret_gn_g, w_out):
    y_prompt = x_prompt
    y_sample = x_sample
    ks, vs, ss = [], [], []
    for l in range(DEPTH):
        lp = {"norm_g": norm_g[l], "w_in": w_in[l], "qn_g": qn_g[l], "kn_g": kn_g[l], "rpb": rpb[l],
              "conv_w": conv_w[l], "conv_b": conv_b[l], "cln_g": cln_g[l], "cln_b": cln_b[l],
              "w_pw": w_pw[l], "ret_decay_logit": ret_decay_logit[l], "ret_dir_scale": ret_dir_scale[l],
              "ret_gn_g": ret_gn_g[l], "w_out": w_out[l]}
        sh, sc, gt = modulation(c_ctx, w_mod[l], b_mod[l])
        y_prompt, (k_l, v_l, s_l) = mixer_layer(y_prompt, sh, sc, gt, lp)
        ks.append(k_l)
        vs.append(v_l)
        ss.append(s_l)
        sh, sc, gt = modulation(c, w_mod[l], b_mod[l])
        y_sample, _ = mixer_layer(y_sample, sh[:, None], sc[:, None], gt[:, None], lp,
                                  ctx=(cache_k[:, l], cache_v[:, l], state_ret[:, l]))
    new_cache_k = jnp.stack(ks, axis=1)
    new_cache_v = jnp.stack(vs, axis=1)
    new_state_ret = jnp.stack(ss, axis=1)
    return (y_prompt, y_sample, new_cache_k, new_cache_v, new_state_ret)
```

```python
import functools

import jax
import jax.numpy as jnp
from jax import lax
from jax.experimental import pallas as pl
from jax.experimental.pallas import tpu as pltpu

F32 = jnp.float32
BF16 = jnp.bfloat16

GRID_W = 64
WIN_H = 8
WIN_W = 16
CONV_K = 31
HEAD_DIM_A = 128
HEAD_DIM_C = 256
ROPE_BASE = 10000.0
EPS = 1e-6
NEG_INF = -1e30

RET_CHUNK = 256
CONV_HALO = 16
COND_ROWS = 8
MIB = 1024 * 1024


def _cparams(n_axes, vmem_mib):
    return pltpu.CompilerParams(
        dimension_semantics=("arbitrary",) * n_axes,
        vmem_limit_bytes=vmem_mib * MIB,
    )


def _silu(x):
    return x * jax.nn.sigmoid(x)


def _dot(a, b):
    return jnp.dot(a, b, preferred_element_type=F32)


def _dot_nt(a, b):
    return lax.dot_general(a, b, (((1,), (1,)), ((), ())), preferred_element_type=F32)


def _rms(x, g):
    return x * lax.rsqrt(jnp.mean(x * x, axis=-1, keepdims=True) + EPS) * g


def _mod_kernel(cond_ref, w_ref, b_ref, o_ref):
    s = _silu(cond_ref[...]).astype(BF16)
    o_ref[0] = _dot(s, w_ref[0].astype(BF16)) + b_ref[0]


def _modulation(cond, w_mod, b_mod):
    depth, d, n = w_mod.shape
    tn = 512
    return pl.pallas_call(
        _mod_kernel,
        grid=(depth, n // tn),
        in_specs=[
            pl.BlockSpec((COND_ROWS, d), lambda l, j: (0, 0)),
            pl.BlockSpec((1, d, tn), lambda l, j: (l, 0, j)),
            pl.BlockSpec((1, 1, tn), lambda l, j: (l, 0, j)),
        ],
        out_specs=pl.BlockSpec((1, COND_ROWS, tn), lambda l, j: (l, 0, j)),
        out_shape=jax.ShapeDtypeStruct((depth, COND_ROWS, n), F32),
        compiler_params=_cparams(2, 40),
        name="modulation",
    )(cond, w_mod, b_mod.reshape(depth, 1, n))


def _norm_mod_kernel(x_ref, g_ref, m_ref, o_ref, *, row0, tiles_per_cond):
    d = x_ref.shape[1]
    row = row0 + pl.program_id(0) // tiles_per_cond
    shift = m_ref[pl.ds(row, 1), 0:d]
    scale = m_ref[pl.ds(row, 1), d:2 * d]
    y = _rms(x_ref[...], g_ref[...])
    o_ref[...] = (y * (1.0 + scale) + shift).astype(BF16)


def _norm_mod(x, g, mod_l, row0, tokens_per_cond):
    t, d = x.shape
    tm = 512
    kern = functools.partial(_norm_mod_kernel, row0=row0, tiles_per_cond=tokens_per_cond // tm)
    return pl.pallas_call(
        kern,
        grid=(t // tm,),
        in_specs=[
            pl.BlockSpec((tm, d), lambda i: (i, 0)),
            pl.BlockSpec((1, d), lambda i: (0, 0)),
            pl.BlockSpec((COND_ROWS, 3 * d), lambda i: (0, 0)),
        ],
        out_specs=pl.BlockSpec((tm, d), lambda i: (i, 0)),
        out_shape=jax.ShapeDtypeStruct((t, d), BF16),
        compiler_params=_cparams(1, 40),
        name="norm_mod",
    )(x, g.reshape(1, d), mod_l)


def _matmul_kernel(a_ref, b_ref, o_ref):
    o_ref[...] = _dot(a_ref[...], b_ref[...])


def _in_proj(h, w):
    t, k = h.shape
    n = w.shape[1]
    tm, tn = 1024, 1024
    return pl.pallas_call(
        _matmul_kernel,
        grid=(t // tm, n // tn),
        in_specs=[
            pl.BlockSpec((tm, k), lambda i, j: (i, 0)),
            pl.BlockSpec((k, tn), lambda i, j: (0, j)),
        ],
        out_specs=pl.BlockSpec((tm, tn), lambda i, j: (i, j)),
        out_shape=jax.ShapeDtypeStruct((t, n), F32),
        compiler_params=_cparams(2, 52),
        name="in_proj",
    )(h, w)


def _ctx_attn_kernel(q_ref, k_ref, v_ref, g_ref, qg_ref, kg_ref, o_ref, ko_ref, vo_ref):
    n_heads = q_ref.shape[1] // HEAD_DIM_A
    scale = HEAD_DIM_A ** -0.5
    vo_ref[0] = v_ref[...]
    for h in range(n_heads):
        hs = slice(h * HEAD_DIM_A, (h + 1) * HEAD_DIM_A)
        q = _rms(q_ref[:, hs], qg_ref[...])
        k = _rms(k_ref[:, hs], kg_ref[...])
        ko_ref[0, :, hs] = k
        s = _dot_nt(q.astype(BF16), k.astype(BF16)) * scale
        p = jnp.exp(s - jnp.max(s, axis=-1, keepdims=True))
        o = _dot(p.astype(BF16), v_ref[:, hs].astype(BF16)) / jnp.sum(p, axis=-1, keepdims=True)
        o_ref[:, hs] = (o * _silu(g_ref[:, hs])).astype(BF16)


def _ctx_attention(z, qg, kg, batch, seq, w_a):
    t = z.shape[0]
    col = lambda c: (lambda b: (b, c))
    zspec = lambda c: pl.BlockSpec((seq, w_a), col(c))
    gspec = pl.BlockSpec((1, HEAD_DIM_A), lambda b: (0, 0))
    cache_spec = pl.BlockSpec((1, seq, w_a), lambda b: (b, 0, 0))
    return pl.pallas_call(
        _ctx_attn_kernel,
        grid=(batch,),
        in_specs=[zspec(0), zspec(1), zspec(2), zspec(3), gspec, gspec],
        out_specs=[pl.BlockSpec((seq, w_a), lambda b: (b, 0)), cache_spec, cache_spec],
        out_shape=[
            jax.ShapeDtypeStruct((t, w_a), BF16),
            jax.ShapeDtypeStruct((batch, seq, w_a), F32),
            jax.ShapeDtypeStruct((batch, seq, w_a), F32),
        ],
        compiler_params=_cparams(1, 40),
        name="ctx_attention",
    )(z, z, z, z, qg.reshape(1, -1), kg.reshape(1, -1))


def _swap_halves_32(x):
    lane = lax.broadcasted_iota(jnp.int32, x.shape, 1)
    return jnp.where((lane & 32) == 0, pltpu.roll(x, 96, 1), pltpu.roll(x, 32, 1))


def _na_prep_kernel(q_ref, k_ref, v_ref, cos_ref, sin_ref, qg_ref, kg_ref, qo_ref, ko_ref, vo_ref):
    n_heads = q_ref.shape[1] // HEAD_DIM_A
    cos = cos_ref[...]
    sin = sin_ref[...]
    vo_ref[...] = v_ref[...].astype(BF16)
    for h in range(n_heads):
        hs = slice(h * HEAD_DIM_A, (h + 1) * HEAD_DIM_A)
        q = _rms(q_ref[:, hs], qg_ref[...])
        k = _rms(k_ref[:, hs], kg_ref[...])
        qo_ref[:, hs] = (q * cos + _swap_halves_32(q) * sin).astype(BF16)
        ko_ref[:, hs] = (k * cos + _swap_halves_32(k) * sin).astype(BF16)


def _na_prep(z, cos, sin, qg, kg, seq, w_a):
    t = z.shape[0]
    tm = 256
    per_seq = seq // tm
    zspec = lambda c: pl.BlockSpec((tm, w_a), lambda i: (i, c))
    tab = pl.BlockSpec((tm, HEAD_DIM_A), lambda i: (i % per_seq, 0))
    gspec = pl.BlockSpec((1, HEAD_DIM_A), lambda i: (0, 0))
    ospec = pl.BlockSpec((tm, w_a), lambda i: (i, 0))
    oshape = jax.ShapeDtypeStruct((t, w_a), BF16)
    return pl.pallas_call(
        _na_prep_kernel,
        grid=(t // tm,),
        in_specs=[zspec(0), zspec(1), zspec(2), tab, tab, gspec, gspec],
        out_specs=[ospec, ospec, ospec],
        out_shape=[oshape, oshape, oshape],
        compiler_params=_cparams(1, 40),
        name="na_prep",
    )(z, z, z, cos, sin, qg.reshape(1, -1), kg.reshape(1, -1))


def _rpb_table_kernel(rpb_ref, o_ref):
    n = o_ref.shape[1]
    lane = lax.broadcasted_iota(jnp.int32, (1, n), 1)
    qcol = lane >> (GRID_W.bit_length() - 1)
    kcol = lane & (GRID_W - 1)
    dc = jnp.clip(kcol - qcol, 1 - WIN_W, WIN_W - 1) + (WIN_W - 1)
    c0 = jnp.clip(qcol - WIN_W // 2, 0, GRID_W - WIN_W)
    in_win = (kcol >= c0) & (kcol < c0 + WIN_W)
    r = rpb_ref[...]
    acc = jnp.zeros(o_ref.shape, F32)
    for j in range(2 * WIN_W - 1):
        acc = jnp.where(dc == j, r[:, j:j + 1], acc)
    o_ref[...] = jnp.where(in_win, acc, NEG_INF)


def _rpb_table(rpb_l, kh):
    n_heads, n_dr, n_dc = rpb_l.shape
    flat = pl.pallas_call(
        _rpb_table_kernel,
        out_shape=jax.ShapeDtypeStruct((n_heads * n_dr, GRID_W * GRID_W), F32),
        name="rpb_table",
    )(rpb_l.reshape(n_heads * n_dr, n_dc))
    b = flat.reshape(n_heads, n_dr, GRID_W, GRID_W)
    slabs = jnp.stack([b[:, d:d + kh] for d in range(WIN_H)], axis=0)
    return slabs.transpose(0, 1, 3, 2, 4).reshape(WIN_H, n_heads, GRID_W, kh * GRID_W)


def _na_kernel(q_ref, k_ref, v_ref, kc_ref, vc_ref, g_ref, b_ref, o_ref, kcb_ref, vcb_ref,
               *, rows, kh, rows_per_step, heads_per_step):
    scale = HEAD_DIM_A ** -0.5
    rb = pl.program_id(2)
    kcb_ref[...] = kc_ref[...].astype(BF16)
    vcb_ref[...] = vc_ref[...].astype(BF16)

    def row_body(rr, carry):
        r = rb * rows_per_step + rr
        r0 = jnp.clip(r - kh // 2, 0, rows - kh)
        dr0 = r0 - r + (WIN_H - 1)
        qs = pl.ds(pl.multiple_of(rr * GRID_W, GRID_W), GRID_W)
        ks = pl.ds(pl.multiple_of(r0 * GRID_W, GRID_W), kh * GRID_W)
        for hh in range(heads_per_step):
            hs = slice(hh * HEAD_DIM_A, (hh + 1) * HEAD_DIM_A)
            q = q_ref[qs, hs]
            s_loc = _dot_nt(q, k_ref[ks, hs]) * scale + b_ref[dr0, hh]
            s_ctx = _dot_nt(q, kcb_ref[:, hs]) * scale
            m = jnp.maximum(jnp.max(s_loc, axis=-1, keepdims=True), jnp.max(s_ctx, axis=-1, keepdims=True))
            p_loc = jnp.exp(s_loc - m)
            p_ctx = jnp.exp(s_ctx - m)
            den = jnp.sum(p_loc, axis=-1, keepdims=True) + jnp.sum(p_ctx, axis=-1, keepdims=True)
            o = (_dot(p_loc.astype(BF16), v_ref[ks, hs]) + _dot(p_ctx.astype(BF16), vcb_ref[:, hs])) / den
            o_ref[qs, hs] = (o * _silu(g_ref[qs, hs])).astype(BF16)
        return carry

    lax.fori_loop(0, rows_per_step, row_body, 0)


def _nbr_attention(qr, kr, vb, cache_k, cache_v, layer, z, bias, batch, seq, w_a):
    t = qr.shape[0]
    rows = seq // GRID_W
    kh = min(WIN_H, rows)
    rows_per_step = 8
    heads_per_step = 4
    wb = heads_per_step * HEAD_DIM_A
    tq = rows_per_step * GRID_W
    steps_per_seq = seq // tq
    lc = cache_k.shape[2]
    ga_col0 = (3 * w_a) // wb
    kern = functools.partial(_na_kernel, rows=rows, kh=kh, rows_per_step=rows_per_step,
                             heads_per_step=heads_per_step)
    qspec = pl.BlockSpec((tq, wb), lambda b, g, r: (b * steps_per_seq + r, g))
    kvspec = pl.BlockSpec((seq, wb), lambda b, g, r: (b, g))
    cspec = pl.BlockSpec((None, None, lc, wb), lambda b, g, r: (b, layer, 0, g))
    return pl.pallas_call(
        kern,
        grid=(batch, w_a // wb, steps_per_seq),
        in_specs=[
            qspec, kvspec, kvspec, cspec, cspec,
            pl.BlockSpec((tq, wb), lambda b, g, r: (b * steps_per_seq + r, ga_col0 + g)),
            pl.BlockSpec((WIN_H, heads_per_step, GRID_W, kh * GRID_W), lambda b, g, r: (0, g, 0, 0)),
        ],
        out_specs=qspec,
        out_shape=jax.ShapeDtypeStruct((t, w_a), BF16),
        scratch_shapes=[pltpu.VMEM((lc, wb), BF16), pltpu.VMEM((lc, wb), BF16)],
        compiler_params=_cparams(3, 48),
        name="nbr_attention",
    )(qr, kr, vb, cache_k, cache_v, z, bias)


def _conv_kernel(u_ref, gl_ref, up_ref, glp_ref, un_ref, gln_ref, gb_ref, cw_ref, cb_ref, lng_ref, lnb_ref,
                 wpw_ref, o_ref, hp_ref, hs_ref, acc_ref, *, tiles_per_seq):
    tl, wb = u_ref.shape
    i = pl.program_id(0)
    first = (i % tiles_per_seq) == 0
    last = (i % tiles_per_seq) == tiles_per_seq - 1
    halo = CONV_HALO
    hp_ref[0:halo, :] = jnp.where(first, 0.0, up_ref[...] * jax.nn.sigmoid(glp_ref[...]))
    hp_ref[halo:halo + tl, :] = u_ref[...] * jax.nn.sigmoid(gl_ref[...])
    hp_ref[halo + tl:2 * halo + tl, :] = jnp.where(last, 0.0, un_ref[...] * jax.nn.sigmoid(gln_ref[...]))
    span = hs_ref.shape[1]
    for s in range(8):
        hs_ref[s] = hp_ref[s:s + span, :]

    rt = 64
    lanes = 128
    off = halo - CONV_K // 2

    def row_body(ri, carry):
        r0 = pl.multiple_of(ri * rt, rt)
        for cc in range(wb // lanes):
            cs = slice(cc * lanes, (cc + 1) * lanes)
            acc = jnp.broadcast_to(cb_ref[:, cs], (rt, lanes))
            for k in range(CONV_K):
                a, s = divmod(k + off, 8)
                acc = acc + hs_ref[s, pl.ds(pl.multiple_of(r0 + 8 * a, 8), rt), cs] * cw_ref[k:k + 1, cs]
            acc_ref[pl.ds(r0, rt), cs] = acc
        return carry

    lax.fori_loop(0, tl // rt, row_body, 0)

    y = acc_ref[...]
    mu = jnp.mean(y, axis=-1, keepdims=True)
    yc = y - mu
    var = jnp.mean(yc * yc, axis=-1, keepdims=True)
    yn = yc * lax.rsqrt(var + EPS) * lng_ref[...] + lnb_ref[...]
    ob = _dot(_silu(yn).astype(BF16), wpw_ref[...])
    o_ref[...] = (ob * _silu(gb_ref[...])).astype(BF16)


def _conformer(z, conv_w, conv_b, ln_g, ln_b, w_pw, seq, w_b, col0):
    t = z.shape[0]
    tl = min(seq, 256)
    tiles_per_seq = seq // tl
    hb = tl // CONV_HALO
    n_hb = t // CONV_HALO
    c_u, c_gl, c_gb = col0 // w_b, col0 // w_b + 1, col0 // w_b + 2
    main = lambda c: pl.BlockSpec((tl, w_b), lambda i: (i, c))
    prev = lambda c: pl.BlockSpec((CONV_HALO, w_b), lambda i: (jnp.maximum(i * hb - 1, 0), c))
    nxt = lambda c: pl.BlockSpec((CONV_HALO, w_b), lambda i: (jnp.minimum((i + 1) * hb, n_hb - 1), c))
    row = pl.BlockSpec((1, w_b), lambda i: (0, 0))
    span = tl + 2 * CONV_HALO - 8
    kern = functools.partial(_conv_kernel, tiles_per_seq=tiles_per_seq)
    return pl.pallas_call(
        kern,
        grid=(t // tl,),
        in_specs=[
            main(c_u), main(c_gl), prev(c_u), prev(c_gl), nxt(c_u), nxt(c_gl), main(c_gb),
            pl.BlockSpec((CONV_K, w_b), lambda i: (0, 0)), row, row, row,
            pl.BlockSpec((w_b, w_b), lambda i: (0, 0)),
        ],
        out_specs=pl.BlockSpec((tl, w_b), lambda i: (i, 0)),
        out_shape=jax.ShapeDtypeStruct((t, w_b), BF16),
        scratch_shapes=[
            pltpu.VMEM((tl + 2 * CONV_HALO, w_b), F32),
            pltpu.VMEM((8, span, w_b), F32),
            pltpu.VMEM((tl, w_b), F32),
        ],
        compiler_params=_cparams(1, 48),
        name="conformer",
    )(z, z, z, z, z, z, z, conv_w, conv_b.reshape(1, -1), ln_g.reshape(1, -1), ln_b.reshape(1, -1), w_pw)


def _swap_halves_128(x):
    half = x.shape[1] // 2
    return jnp.concatenate([x[:, half:], x[:, :half]], axis=1)


def _ret_kernel(coef_ref, q_ref, k_ref, v_ref, g_ref, gn_ref, *rest, n_chunks, n_heads, latent):
    if latent:
        cos_ref, sin_ref, s0_ref, o_ref = rest[:4]
        scratch = rest[4:]
    else:
        o_ref, so_ref = rest[:2]
        scratch = rest[2:]
    m_ref, xf_ref, xb_ref, zf_ref, zb_ref, sf_ref, sb_ref, oacc_ref, qs_ref, ks_ref = scratch
    c = RET_CHUNK
    h = pl.program_id(1)
    lgf = coef_ref[h]
    lgb = coef_ref[n_heads + h]
    dsf = coef_ref[2 * n_heads + h]
    dsb = coef_ref[3 * n_heads + h]

    rows = lax.broadcasted_iota(jnp.int32, (c, c), 0).astype(F32)
    cols = lax.broadcasted_iota(jnp.int32, (c, c), 1).astype(F32)
    diff = rows - cols
    m_ref[...] = (jnp.where(diff >= 0, jnp.exp(lgf * jnp.maximum(diff, 0.0)), 0.0) * dsf
                  + jnp.where(diff <= 0, jnp.exp(lgb * jnp.maximum(-diff, 0.0)), 0.0) * dsb)
    xf_ref[...] = dsf * jnp.exp(lgf * (rows + 1.0))
    xb_ref[...] = dsb * jnp.exp(lgb * (c - rows))
    zf_ref[...] = jnp.exp(lgf * (c - 1.0 - rows))
    zb_ref[...] = jnp.exp(lgb * rows)
    chunk_len = jnp.full((1, c), float(c), F32)
    gcf = jnp.exp(lgf * chunk_len)
    gcb = jnp.exp(lgb * chunk_len)

    if latent:
        sf_ref[...] = s0_ref[0]
        sb_ref[...] = s0_ref[1]
    else:
        sf_ref[...] = jnp.zeros((c, c), F32)
        sb_ref[...] = jnp.zeros((c, c), F32)
    use_cross = latent or n_chunks > 1
    k_scale = HEAD_DIM_C ** -0.5

    def forward(ci, carry):
        sl = pl.ds(pl.multiple_of(ci * c, c), c)
        q = q_ref[sl, :]
        k = k_ref[sl, :] * k_scale
        if latent:
            cos = cos_ref[sl, :]
            sin = sin_ref[sl, :]
            q = q * cos + _swap_halves_128(q) * sin
            k = k * cos + _swap_halves_128(k) * sin
        qb = q.astype(BF16)
        vb = v_ref[sl, :].astype(BF16)
        qs_ref[sl, :] = qb
        ks_ref[sl, :] = k
        a = _dot_nt(qb, k.astype(BF16))
        o = _dot((a * m_ref[...]).astype(BF16), vb)
        if use_cross:
            o = o + _dot(qb, sf_ref[...].astype(BF16)) * xf_ref[...]
        oacc_ref[sl, :] = o
        sf_ref[...] = gcf * sf_ref[...] + _dot((k * zf_ref[...]).T.astype(BF16), vb)
        return carry

    def backward(cj, carry):
        ci = n_chunks - 1 - cj
        sl = pl.ds(pl.multiple_of(ci * c, c), c)
        k = ks_ref[sl, :]
        vb = v_ref[sl, :].astype(BF16)
        o = oacc_ref[sl, :]
        if use_cross:
            o = o + _dot(qs_ref[sl, :], sb_ref[...].astype(BF16)) * xb_ref[...]
        sb_ref[...] = gcb * sb_ref[...] + _dot((k * zb_ref[...]).T.astype(BF16), vb)
        o_ref[sl, :] = (_rms(o, gn_ref[...]) * _silu(g_ref[sl, :])).astype(BF16)
        return carry

    lax.fori_loop(0, n_chunks, forward, 0)
    lax.fori_loop(0, n_chunks, backward, 0)
    if not latent:
        so_ref[0, 0, 0] = sf_ref[...]
        so_ref[0, 1, 0] = sb_ref[...]


def _retention(z, coef, gn_g, batch, seq, n_heads, col0, cos=None, sin=None, state=None, layer=None):
    t = z.shape[0]
    dh = HEAD_DIM_C
    latent = state is not None
    n_chunks = seq // RET_CHUNK
    cb = col0 // dh
    zspec = lambda seg: pl.BlockSpec((seq, dh), lambda b, h: (b, cb + seg * n_heads + h))
    in_specs = [
        pl.BlockSpec(memory_space=pltpu.SMEM),
        zspec(0), zspec(1), zspec(2), zspec(3),
        pl.BlockSpec((1, dh), lambda b, h: (0, h)),
    ]
    args = [coef, z, z, z, z, gn_g.reshape(1, -1)]
    ospec = pl.BlockSpec((seq, dh), lambda b, h: (b, h))
    oshape = jax.ShapeDtypeStruct((t, n_heads * dh), BF16)
    if latent:
        tab = pl.BlockSpec((seq, dh), lambda b, h: (0, 0))
        in_specs += [tab, tab, pl.BlockSpec((None, None, 2, None, dh, dh), lambda b, h: (b, layer, 0, h, 0, 0))]
        args += [cos, sin, state]
        out_specs, out_shape = ospec, oshape
    else:
        out_specs = [ospec, pl.BlockSpec((1, 2, 1, dh, dh), lambda b, h: (b, 0, h, 0, 0))]
        out_shape = [oshape, jax.ShapeDtypeStruct((batch, 2, n_heads, dh, dh), F32)]
    sq = lambda: pltpu.VMEM((RET_CHUNK, RET_CHUNK), F32)
    kern = functools.partial(_ret_kernel, n_chunks=n_chunks, n_heads=n_heads, latent=latent)
    return pl.pallas_call(
        kern,
        grid=(batch, n_heads),
        in_specs=in_specs,
        out_specs=out_specs,
        out_shape=out_shape,
        scratch_shapes=[sq(), sq(), sq(), sq(), sq(), sq(), sq(),
                        pltpu.VMEM((seq, dh), F32), pltpu.VMEM((seq, dh), BF16), pltpu.VMEM((seq, dh), F32)],
        compiler_params=_cparams(2, 48),
        name="retention",
    )(*args)


def _out_proj_kernel(oa_ref, ob_ref, oc_ref, w_ref, x_ref, m_ref, y_ref, *, row0, tiles_per_cond):
    wa = oa_ref.shape[1]
    wb = ob_ref.shape[1]
    row = row0 + pl.program_id(0) // tiles_per_cond
    acc = _dot(oa_ref[...], w_ref[0:wa, :])
    acc = acc + _dot(ob_ref[...], w_ref[wa:wa + wb, :])
    acc = acc + _dot(oc_ref[...], w_ref[wa + wb:, :])
    y_ref[...] = x_ref[...] + m_ref[pl.ds(row, 1), :] * acc


def _out_proj(oa, ob, oc, w, x, mod_l, row0, tokens_per_cond):
    t, d = x.shape
    tm, tn = 1024, 512
    gate_col0 = (2 * d) // tn
    kern = functools.partial(_out_proj_kernel, row0=row0, tiles_per_cond=tokens_per_cond // tm)
    act = lambda a: pl.BlockSpec((tm, a.shape[1]), lambda i, j: (i, 0))
    return pl.pallas_call(
        kern,
        grid=(t // tm, d // tn),
        in_specs=[
            act(oa), act(ob), act(oc),
            pl.BlockSpec((d, tn), lambda i, j: (0, j)),
            pl.BlockSpec((tm, tn), lambda i, j: (i, j)),
            pl.BlockSpec((COND_ROWS, tn), lambda i, j: (0, gate_col0 + j)),
        ],
        out_specs=pl.BlockSpec((tm, tn), lambda i, j: (i, j)),
        out_shape=jax.ShapeDtypeStruct((t, d), F32),
        compiler_params=_cparams(2, 48),
        name="out_proj",
    )(oa, ob, oc, w, x, mod_l)


def _rope_angles(pos, n_dims):
    half = n_dims // 2
    freqs = ROPE_BASE ** (-jnp.arange(half, dtype=F32) / half)
    return pos.astype(F32)[:, None] * freqs[None, :]


def _axial_tables(seq):
    t = jnp.arange(seq)
    ar = _rope_angles(t // GRID_W, HEAD_DIM_A // 2)
    ac = _rope_angles(t % GRID_W, HEAD_DIM_A // 2)
    cos = jnp.concatenate([jnp.cos(ar), jnp.cos(ar), jnp.cos(ac), jnp.cos(ac)], -1)
    sin = jnp.concatenate([-jnp.sin(ar), jnp.sin(ar), -jnp.sin(ac), jnp.sin(ac)], -1)
    return cos, sin


def _ret_tables(seq):
    ang = _rope_angles(jnp.arange(seq), HEAD_DIM_C)
    cos = jnp.concatenate([jnp.cos(ang), jnp.cos(ang)], -1)
    sin = jnp.concatenate([-jnp.sin(ang), jnp.sin(ang)], -1)
    return cos, sin


def kernel(x_prompt, x_sample, cache_k, cache_v, state_ret, c, c_ctx, norm_g, w_mod, b_mod, w_in, qn_g, kn_g,
           rpb, conv_w, conv_b, cln_g, cln_b, w_pw, ret_decay_logit, ret_dir_scale, ret_gn_g, w_out):
    batch, seq, d = x_prompt.shape
    dec_batch, dec_seq, _ = x_sample.shape
    depth = w_in.shape[0]
    n_heads_a = cache_k.shape[3]
    w_a = n_heads_a * HEAD_DIM_A
    w_b = w_pw.shape[1]
    n_heads_c = state_ret.shape[3]
    w_c = n_heads_c * HEAD_DIM_C
    col_b = 4 * w_a
    col_c = col_b + 3 * w_b
    assert w_in.shape[2] == col_c + 4 * w_c and w_a + w_b + w_c == d
    assert 1 + dec_batch <= COND_ROWS
    t_ctx = batch * seq
    t_lat = dec_batch * dec_seq
    rows = dec_seq // GRID_W
    kh = min(WIN_H, rows)

    cond = jnp.concatenate([c_ctx[None], c, jnp.zeros((COND_ROWS - 1 - dec_batch, d), F32)], axis=0)
    mod = _modulation(cond, w_mod, b_mod)
    w_in_b = w_in.astype(BF16)
    w_out_b = w_out.astype(BF16)
    w_pw_b = w_pw.astype(BF16)
    cache_k2 = cache_k.reshape(dec_batch, depth, -1, w_a)
    cache_v2 = cache_v.reshape(dec_batch, depth, -1, w_a)
    ax_cos, ax_sin = _axial_tables(dec_seq)
    rt_cos, rt_sin = _ret_tables(dec_seq)
    log_gamma = jax.nn.log_sigmoid(ret_decay_logit.astype(F32))
    ret_coef = jnp.concatenate([log_gamma, ret_dir_scale.astype(F32)], axis=1).reshape(depth, -1)

    xp = x_prompt.reshape(t_ctx, d)
    xs = x_sample.reshape(t_lat, d)
    ks, vs, ss = [], [], []
    for l in range(depth):
        conv_args = (conv_w[l], conv_b[l], cln_g[l], cln_b[l], w_pw_b[l])
        h = _norm_mod(xp, norm_g[l], mod[l], 0, t_ctx)
        z = _in_proj(h, w_in_b[l])
        oa, k_l, v_l = _ctx_attention(z, qn_g[l], kn_g[l], batch, seq, w_a)
        ob = _conformer(z, *conv_args, seq, w_b, col_b)
        oc, s_l = _retention(z, ret_coef[l], ret_gn_g[l], batch, seq, n_heads_c, col_c)
        xp = _out_proj(oa, ob, oc, w_out_b[l], xp, mod[l], 0, t_ctx)
        ks.append(k_l.reshape(batch, seq, n_heads_a, HEAD_DIM_A))
        vs.append(v_l.reshape(batch, seq, n_heads_a, HEAD_DIM_A))
        ss.append(s_l)
        h = _norm_mod(xs, norm_g[l], mod[l], 1, dec_seq)
        z = _in_proj(h, w_in_b[l])
        qr, kr, vb = _na_prep(z, ax_cos, ax_sin, qn_g[l], kn_g[l], dec_seq, w_a)
        bias = _rpb_table(rpb[l], kh)
        oa = _nbr_attention(qr, kr, vb, cache_k2, cache_v2, l, z, bias, dec_batch, dec_seq, w_a)
        ob = _conformer(z, *conv_args, dec_seq, w_b, col_b)
        oc = _retention(z, ret_coef[l], ret_gn_g[l], dec_batch, dec_seq, n_heads_c, col_c,
                        cos=rt_cos, sin=rt_sin, state=state_ret, layer=l)
        xs = _out_proj(oa, ob, oc, w_out_b[l], xs, mod[l], 1, dec_seq)
    return (xp.reshape(batch, seq, d), xs.reshape(dec_batch, dec_seq, d),
            jnp.stack(ks, axis=1), jnp.stack(vs, axis=1), jnp.stack(ss, axis=1))
```

```python
import functools

import jax
import jax.numpy as jnp
from jax import lax
from jax.experimental import pallas as pl
from jax.experimental.pallas import tpu as pltpu

F32 = jnp.float32
BF16 = jnp.bfloat16

GRID_W = 64
WIN_H = 8
WIN_W = 16
CONV_K = 31
HEAD_DIM_A = 128
HEAD_DIM_C = 256
ROPE_BASE = 10000.0
EPS = 1e-6
NEG_INF = -1e30

RET_CHUNK = 256
CONV_HALO = 16
COND_ROWS = 8
MIB = 1024 * 1024


def _cparams(n_axes, vmem_mib):
    return pltpu.CompilerParams(
        dimension_semantics=("arbitrary",) * n_axes,
        vmem_limit_bytes=vmem_mib * MIB,
    )


def _silu(x):
    return x * jax.nn.sigmoid(x)


def _dot(a, b):
    return jnp.dot(a, b, preferred_element_type=F32)


def _dot_nt(a, b):
    return lax.dot_general(a, b, (((1,), (1,)), ((), ())), preferred_element_type=F32)


def _rms(x, g):
    return x * lax.rsqrt(jnp.mean(x * x, axis=-1, keepdims=True) + EPS) * g


def _mod_kernel(cond_ref, w_ref, b_ref, o_ref):
    s = _silu(cond_ref[...]).astype(BF16)
    o_ref[0] = _dot(s, w_ref[0].astype(BF16)) + b_ref[0]


def _modulation(cond, w_mod, b_mod):
    depth, d, n = w_mod.shape
    tn = 512
    return pl.pallas_call(
        _mod_kernel,
        grid=(depth, n // tn),
        in_specs=[
            pl.BlockSpec((COND_ROWS, d), lambda l, j: (0, 0)),
            pl.BlockSpec((1, d, tn), lambda l, j: (l, 0, j)),
            pl.BlockSpec((1, 1, tn), lambda l, j: (l, 0, j)),
        ],
        out_specs=pl.BlockSpec((1, COND_ROWS, tn), lambda l, j: (l, 0, j)),
        out_shape=jax.ShapeDtypeStruct((depth, COND_ROWS, n), F32),
        compiler_params=_cparams(2, 40),
        name="modulation",
    )(cond, w_mod, b_mod.reshape(depth, 1, n))


def _norm_mod_kernel(x_ref, g_ref, m_ref, o_ref, *, row0, tiles_per_cond):
    d = x_ref.shape[1]
    row = row0 + pl.program_id(0) // tiles_per_cond
    shift = m_ref[pl.ds(row, 1), 0:d]
    scale = m_ref[pl.ds(row, 1), d:2 * d]
    y = _rms(x_ref[...], g_ref[...])
    o_ref[...] = (y * (1.0 + scale) + shift).astype(BF16)


def _norm_mod(x, g, mod, layer, row0, tokens_per_cond):
    t, d = x.shape
    tm = 512
    kern = functools.partial(_norm_mod_kernel, row0=row0, tiles_per_cond=tokens_per_cond // tm)
    return pl.pallas_call(
        kern,
        grid=(t // tm,),
        in_specs=[
            pl.BlockSpec((tm, d), lambda i: (i, 0)),
            pl.BlockSpec((1, d), lambda i: (0, 0)),
            pl.BlockSpec((None, COND_ROWS, 3 * d), lambda i: (layer, 0, 0)),
        ],
        out_specs=pl.BlockSpec((tm, d), lambda i: (i, 0)),
        out_shape=jax.ShapeDtypeStruct((t, d), BF16),
        compiler_params=_cparams(1, 40),
        name="norm_mod",
    )(x, g.reshape(1, d), mod)


def _matmul_kernel(a_ref, b_ref, o_ref):
    o_ref[...] = _dot(a_ref[...], b_ref[...])


def _in_proj(h, w, layer):
    t, k = h.shape
    n = w.shape[2]
    tm, tn = 1024, 1024
    return pl.pallas_call(
        _matmul_kernel,
        grid=(t // tm, n // tn),
        in_specs=[
            pl.BlockSpec((tm, k), lambda i, j: (i, 0)),
            pl.BlockSpec((None, k, tn), lambda i, j: (layer, 0, j)),
        ],
        out_specs=pl.BlockSpec((tm, tn), lambda i, j: (i, j)),
        out_shape=jax.ShapeDtypeStruct((t, n), F32),
        compiler_params=_cparams(2, 52),
        name="in_proj",
    )(h, w)


def _ctx_attn_kernel(q_ref, k_ref, v_ref, g_ref, qg_ref, kg_ref, *rest):
    o_ref, ko_ref, vo_ref = rest[-3:]
    n_heads = q_ref.shape[1] // HEAD_DIM_A
    scale = HEAD_DIM_A ** -0.5
    vo_ref[...] = v_ref[...]
    for h in range(n_heads):
        hs = slice(h * HEAD_DIM_A, (h + 1) * HEAD_DIM_A)
        q = _rms(q_ref[:, hs], qg_ref[...])
        k = _rms(k_ref[:, hs], kg_ref[...])
        ko_ref[:, hs] = k
        s = _dot_nt(q.astype(BF16), k.astype(BF16)) * scale
        p = jnp.exp(s - jnp.max(s, axis=-1, keepdims=True))
        o = _dot(p.astype(BF16), v_ref[:, hs].astype(BF16)) / jnp.sum(p, axis=-1, keepdims=True)
        o_ref[:, hs] = (o * _silu(g_ref[:, hs])).astype(BF16)


def _ctx_attention(z, qg, kg, batch, seq, w_a, depth, layer, caches=None):
    t = z.shape[0]
    col = lambda c: (lambda b: (b, c))
    zspec = lambda c: pl.BlockSpec((seq, w_a), col(c))
    gspec = pl.BlockSpec((1, HEAD_DIM_A), lambda b: (0, 0))
    cache_spec = pl.BlockSpec((None, None, seq, w_a), lambda b: (b, layer, 0, 0))
    cache_shape = jax.ShapeDtypeStruct((batch, depth, seq, w_a), F32)
    args = [z, z, z, z, qg.reshape(1, -1), kg.reshape(1, -1)]
    in_specs = [zspec(0), zspec(1), zspec(2), zspec(3), gspec, gspec]
    aliases = {}
    if caches is not None:
        aliases = {len(args): 1, len(args) + 1: 2}
        args += list(caches)
        in_specs += [pl.BlockSpec(memory_space=pl.ANY)] * 2
    return pl.pallas_call(
        _ctx_attn_kernel,
        grid=(batch,),
        in_specs=in_specs,
        out_specs=[pl.BlockSpec((seq, w_a), lambda b: (b, 0)), cache_spec, cache_spec],
        out_shape=[jax.ShapeDtypeStruct((t, w_a), BF16), cache_shape, cache_shape],
        input_output_aliases=aliases,
        compiler_params=_cparams(1, 40),
        name="ctx_attention",
    )(*args)


def _swap_halves_32(x):
    lane = lax.broadcasted_iota(jnp.int32, x.shape, 1)
    return jnp.where((lane & 32) == 0, pltpu.roll(x, 96, 1), pltpu.roll(x, 32, 1))


def _na_prep_kernel(q_ref, k_ref, v_ref, cos_ref, sin_ref, qg_ref, kg_ref, qo_ref, ko_ref, vo_ref):
    n_heads = q_ref.shape[1] // HEAD_DIM_A
    cos = cos_ref[...]
    sin = sin_ref[...]
    vo_ref[...] = v_ref[...].astype(BF16)
    for h in range(n_heads):
        hs = slice(h * HEAD_DIM_A, (h + 1) * HEAD_DIM_A)
        q = _rms(q_ref[:, hs], qg_ref[...])
        k = _rms(k_ref[:, hs], kg_ref[...])
        qo_ref[:, hs] = (q * cos + _swap_halves_32(q) * sin).astype(BF16)
        ko_ref[:, hs] = (k * cos + _swap_halves_32(k) * sin).astype(BF16)


def _na_prep(z, cos, sin, qg, kg, seq, w_a):
    t = z.shape[0]
    tm = 256
    per_seq = seq // tm
    zspec = lambda c: pl.BlockSpec((tm, w_a), lambda i: (i, c))
    tab = pl.BlockSpec((tm, HEAD_DIM_A), lambda i: (i % per_seq, 0))
    gspec = pl.BlockSpec((1, HEAD_DIM_A), lambda i: (0, 0))
    ospec = pl.BlockSpec((tm, w_a), lambda i: (i, 0))
    oshape = jax.ShapeDtypeStruct((t, w_a), BF16)
    return pl.pallas_call(
        _na_prep_kernel,
        grid=(t // tm,),
        in_specs=[zspec(0), zspec(1), zspec(2), tab, tab, gspec, gspec],
        out_specs=[ospec, ospec, ospec],
        out_shape=[oshape, oshape, oshape],
        compiler_params=_cparams(1, 40),
        name="na_prep",
    )(z, z, z, cos, sin, qg.reshape(1, -1), kg.reshape(1, -1))


def _rpb_table_kernel(rpb_ref, o_ref):
    n = o_ref.shape[1]
    lane = lax.broadcasted_iota(jnp.int32, (1, n), 1)
    qcol = lane >> (GRID_W.bit_length() - 1)
    kcol = lane & (GRID_W - 1)
    dc = jnp.clip(kcol - qcol, 1 - WIN_W, WIN_W - 1) + (WIN_W - 1)
    c0 = jnp.clip(qcol - WIN_W // 2, 0, GRID_W - WIN_W)
    in_win = (kcol >= c0) & (kcol < c0 + WIN_W)
    r = rpb_ref[...]
    acc = jnp.zeros(o_ref.shape, F32)
    for j in range(2 * WIN_W - 1):
        acc = jnp.where(dc == j, r[:, j:j + 1], acc)
    o_ref[...] = jnp.where(in_win, acc, NEG_INF)


def _rpb_table(rpb_l, kh):
    n_heads, n_dr, n_dc = rpb_l.shape
    flat = pl.pallas_call(
        _rpb_table_kernel,
        out_shape=jax.ShapeDtypeStruct((n_heads * n_dr, GRID_W * GRID_W), F32),
        name="rpb_table",
    )(rpb_l.reshape(n_heads * n_dr, n_dc))
    b = flat.reshape(n_heads, n_dr, GRID_W, GRID_W)
    slabs = jnp.stack([b[:, d:d + kh] for d in range(WIN_H)], axis=0)
    return slabs.transpose(0, 1, 3, 2, 4).reshape(WIN_H, n_heads, GRID_W, kh * GRID_W)


NA_QROWS = 4
NA_KROWS = NA_QROWS + WIN_H


def _na_block_plan(rows, kh):
    plans = {}
    for jb in range(rows // NA_QROWS):
        k0 = min(max(NA_QROWS * jb - kh // 2, 0), rows - NA_KROWS)
        plan = []
        for i in range(NA_QROWS):
            r = NA_QROWS * jb + i
            r0 = min(max(r - kh // 2, 0), rows - kh)
            assert 0 <= r0 - k0 <= NA_KROWS - kh
            plan.append((r0 - k0, r0 - r + WIN_H - 1))
        plans.setdefault(tuple(plan), []).append(jb)
    return plans


def _na_kernel(q_ref, k_ref, v_ref, kc_ref, vc_ref, g_ref, t_ref, ts_ref, o_ref, bias_ref, kcb_ref, vcb_ref,
               *, rows, kh, heads_per_step, shifted_dr):
    scale = HEAD_DIM_A ** -0.5
    jb = pl.program_id(1)
    span = kh * GRID_W

    @pl.when(pl.program_id(2) == 0)
    def _assemble():
        for plan, jbs in _na_block_plan(rows, kh).items():
            pred = functools.reduce(jnp.logical_or, [jb == j for j in jbs])

            @pl.when(pred)
            def _():
                for hh in range(heads_per_step):
                    for i, (off, dr0) in enumerate(plan):
                        rs = slice(i * GRID_W, (i + 1) * GRID_W)
                        bias_ref[hh, rs, :] = jnp.full((GRID_W, NA_KROWS * GRID_W), NEG_INF, F32)
                        if off % 2 == 0:
                            bias_ref[hh, rs, off * GRID_W:off * GRID_W + span] = t_ref[dr0, hh]
                        else:
                            assert dr0 == shifted_dr
                            lo = (off - 1) * GRID_W
                            bias_ref[hh, rs, lo:lo + span + 2 * GRID_W] = ts_ref[hh]

    kcb_ref[...] = kc_ref[...].astype(BF16)
    vcb_ref[...] = vc_ref[...].astype(BF16)
    k0 = jnp.clip(NA_QROWS * jb - kh // 2, 0, rows - NA_KROWS)
    ks = pl.ds(pl.multiple_of(k0 * GRID_W, NA_QROWS * GRID_W), NA_KROWS * GRID_W)
    for hh in range(heads_per_step):
        hs = slice(hh * HEAD_DIM_A, (hh + 1) * HEAD_DIM_A)
        q = q_ref[:, hs]
        s_loc = _dot_nt(q, k_ref[ks, hs]) * scale + bias_ref[hh]
        s_ctx = _dot_nt(q, kcb_ref[:, hs]) * scale
        m = jnp.maximum(jnp.max(s_loc, axis=-1, keepdims=True), jnp.max(s_ctx, axis=-1, keepdims=True))
        p_loc = jnp.exp(s_loc - m)
        p_ctx = jnp.exp(s_ctx - m)
        den = jnp.sum(p_loc, axis=-1, keepdims=True) + jnp.sum(p_ctx, axis=-1, keepdims=True)
        o = (_dot(p_loc.astype(BF16), v_ref[ks, hs]) + _dot(p_ctx.astype(BF16), vcb_ref[:, hs])) / den
        o_ref[:, hs] = (o * _silu(g_ref[:, hs])).astype(BF16)


def _nbr_attention(qr, kr, vb, cache_k, cache_v, layer, z, bias, batch, seq, w_a):
    t = qr.shape[0]
    rows = seq // GRID_W
    kh = min(WIN_H, rows)
    assert rows % NA_QROWS == 0 and rows >= NA_KROWS and NA_QROWS % 2 == 0
    heads_per_step = 4
    wb = heads_per_step * HEAD_DIM_A
    tq = NA_QROWS * GRID_W
    steps_per_seq = seq // tq
    lc = cache_k.shape[2]
    ga_col0 = (3 * w_a) // wb
    shifted_dr = WIN_H - 1 - kh // 2
    pad = ((0, 0), (0, 0), (GRID_W, GRID_W))
    bias_shifted = jnp.pad(bias[shifted_dr], pad, constant_values=NEG_INF)
    kern = functools.partial(_na_kernel, rows=rows, kh=kh, heads_per_step=heads_per_step, shifted_dr=shifted_dr)
    qspec = pl.BlockSpec((tq, wb), lambda g, r, b: (b * steps_per_seq + r, g))
    kvspec = pl.BlockSpec((seq, wb), lambda g, r, b: (b, g))
    cspec = pl.BlockSpec((None, None, lc, wb), lambda g, r, b: (b, layer, 0, g))
    return pl.pallas_call(
        kern,
        grid=(w_a // wb, steps_per_seq, batch),
        in_specs=[
            qspec, kvspec, kvspec, cspec, cspec,
            pl.BlockSpec((tq, wb), lambda g, r, b: (b * steps_per_seq + r, ga_col0 + g)),
            pl.BlockSpec((WIN_H, heads_per_step, GRID_W, kh * GRID_W), lambda g, r, b: (0, g, 0, 0)),
            pl.BlockSpec((heads_per_step, GRID_W, (kh + 2) * GRID_W), lambda g, r, b: (g, 0, 0)),
        ],
        out_specs=qspec,
        out_shape=jax.ShapeDtypeStruct((t, w_a), BF16),
        scratch_shapes=[pltpu.VMEM((heads_per_step, tq, NA_KROWS * GRID_W), F32),
                        pltpu.VMEM((lc, wb), BF16), pltpu.VMEM((lc, wb), BF16)],
        compiler_params=_cparams(3, 48),
        name="nbr_attention",
    )(qr, kr, vb, cache_k, cache_v, z, bias, bias_shifted)


def _conv_kernel(u_ref, gl_ref, up_ref, glp_ref, un_ref, gln_ref, gb_ref, cw_ref, cb_ref, lng_ref, lnb_ref,
                 wpw_ref, o_ref, hp_ref, hs_ref, acc_ref, *, tiles_per_seq):
    tl, wb = u_ref.shape
    i = pl.program_id(0)
    first = (i % tiles_per_seq) == 0
    last = (i % tiles_per_seq) == tiles_per_seq - 1
    halo = CONV_HALO
    hp_ref[0:halo, :] = jnp.where(first, 0.0, up_ref[...] * jax.nn.sigmoid(glp_ref[...]))
    hp_ref[halo:halo + tl, :] = u_ref[...] * jax.nn.sigmoid(gl_ref[...])
    hp_ref[halo + tl:2 * halo + tl, :] = jnp.where(last, 0.0, un_ref[...] * jax.nn.sigmoid(gln_ref[...]))
    span = hs_ref.shape[1]
    for s in range(8):
        hs_ref[s] = hp_ref[s:s + span, :]

    rt = 128
    lanes = 128
    off = halo - CONV_K // 2

    for cc in range(wb // lanes):
        cs = slice(cc * lanes, (cc + 1) * lanes)

        def row_body(ri, carry, cs=cs):
            r0 = pl.multiple_of(ri * rt, rt)
            acc = jnp.broadcast_to(cb_ref[:, cs], (rt, lanes))
            for s in range(8):
                taps = [k for k in range(CONV_K) if (k + off) % 8 == s]
                reach = 8 * max((k + off) // 8 for k in taps)
                win = hs_ref[s, pl.ds(r0, rt + reach), cs]
                for k in taps:
                    a = (k + off) // 8
                    acc = acc + win[8 * a:8 * a + rt] * cw_ref[k:k + 1, cs]
            acc_ref[pl.ds(r0, rt), cs] = acc
            return carry

        lax.fori_loop(0, tl // rt, row_body, 0)

    y = acc_ref[...]
    mu = jnp.mean(y, axis=-1, keepdims=True)
    yc = y - mu
    var = jnp.mean(yc * yc, axis=-1, keepdims=True)
    yn = yc * lax.rsqrt(var + EPS) * lng_ref[...] + lnb_ref[...]
    ob = _dot(_silu(yn).astype(BF16), wpw_ref[...])
    o_ref[...] = (ob * _silu(gb_ref[...])).astype(BF16)


def _conformer(z, conv_w, conv_b, ln_g, ln_b, w_pw, layer, seq, w_b, col0):
    t = z.shape[0]
    tl = min(seq, 256)
    tiles_per_seq = seq // tl
    hb = tl // CONV_HALO
    n_hb = t // CONV_HALO
    c_u, c_gl, c_gb = col0 // w_b, col0 // w_b + 1, col0 // w_b + 2
    main = lambda c: pl.BlockSpec((tl, w_b), lambda i: (i, c))
    prev = lambda c: pl.BlockSpec((CONV_HALO, w_b), lambda i: (jnp.maximum(i * hb - 1, 0), c))
    nxt = lambda c: pl.BlockSpec((CONV_HALO, w_b), lambda i: (jnp.minimum((i + 1) * hb, n_hb - 1), c))
    row = pl.BlockSpec((1, w_b), lambda i: (0, 0))
    span = tl + 2 * CONV_HALO - 8
    kern = functools.partial(_conv_kernel, tiles_per_seq=tiles_per_seq)
    return pl.pallas_call(
        kern,
        grid=(t // tl,),
        in_specs=[
            main(c_u), main(c_gl), prev(c_u), prev(c_gl), nxt(c_u), nxt(c_gl), main(c_gb),
            pl.BlockSpec((CONV_K, w_b), lambda i: (0, 0)), row, row, row,
            pl.BlockSpec((None, w_b, w_b), lambda i: (layer, 0, 0)),
        ],
        out_specs=pl.BlockSpec((tl, w_b), lambda i: (i, 0)),
        out_shape=jax.ShapeDtypeStruct((t, w_b), BF16),
        scratch_shapes=[
            pltpu.VMEM((tl + 2 * CONV_HALO, w_b), F32),
            pltpu.VMEM((8, span, w_b), F32),
            pltpu.VMEM((tl, w_b), F32),
        ],
        compiler_params=_cparams(1, 48),
        name="conformer",
    )(z, z, z, z, z, z, z, conv_w, conv_b.reshape(1, -1), ln_g.reshape(1, -1), ln_b.reshape(1, -1), w_pw)


def _swap_halves_128(x):
    half = x.shape[1] // 2
    return jnp.concatenate([x[:, half:], x[:, :half]], axis=1)


def _ret_kernel(coef_ref, q_ref, k_ref, v_ref, g_ref, gn_ref, *rest, n_chunks, n_heads, latent):
    n_scratch = 10
    scratch = rest[-n_scratch:]
    if latent:
        cos_ref, sin_ref, s0_ref, o_ref = rest[:4]
    else:
        o_ref, so_ref = rest[-n_scratch - 2:-n_scratch]
    m_ref, xf_ref, xb_ref, zf_ref, zb_ref, sf_ref, sb_ref, oacc_ref, qs_ref, ks_ref = scratch
    c = RET_CHUNK
    h = pl.program_id(0)
    lgf = coef_ref[h]
    lgb = coef_ref[n_heads + h]
    dsf = coef_ref[2 * n_heads + h]
    dsb = coef_ref[3 * n_heads + h]

    @pl.when(pl.program_id(1) == 0)
    def _tables():
        rows = lax.broadcasted_iota(jnp.int32, (c, c), 0).astype(F32)
        cols = lax.broadcasted_iota(jnp.int32, (c, c), 1).astype(F32)
        diff = rows - cols
        m_ref[...] = (jnp.where(diff >= 0, jnp.exp(lgf * jnp.maximum(diff, 0.0)), 0.0) * dsf
                      + jnp.where(diff <= 0, jnp.exp(lgb * jnp.maximum(-diff, 0.0)), 0.0) * dsb)
        xf_ref[...] = dsf * jnp.exp(lgf * (rows + 1.0))
        xb_ref[...] = dsb * jnp.exp(lgb * (c - rows))
        zf_ref[...] = jnp.exp(lgf * (c - 1.0 - rows))
        zb_ref[...] = jnp.exp(lgb * rows)

    chunk_len = jnp.full((1, c), float(c), F32)
    gcf = jnp.exp(lgf * chunk_len)
    gcb = jnp.exp(lgb * chunk_len)

    if latent:
        sf_ref[...] = s0_ref[0]
        sb_ref[...] = s0_ref[1]
    else:
        sf_ref[...] = jnp.zeros((c, c), F32)
        sb_ref[...] = jnp.zeros((c, c), F32)
    use_cross = latent or n_chunks > 1
    k_scale = HEAD_DIM_C ** -0.5

    def forward(ci, carry):
        sl = pl.ds(pl.multiple_of(ci * c, c), c)
        q = q_ref[sl, :]
        k = k_ref[sl, :] * k_scale
        if latent:
            cos = cos_ref[sl, :]
            sin = sin_ref[sl, :]
            q = q * cos + _swap_halves_128(q) * sin
            k = k * cos + _swap_halves_128(k) * sin
        qb = q.astype(BF16)
        vb = v_ref[sl, :].astype(BF16)
        qs_ref[sl, :] = qb
        ks_ref[sl, :] = k
        a = _dot_nt(qb, k.astype(BF16))
        o = _dot((a * m_ref[...]).astype(BF16), vb)
        if use_cross:
            o = o + _dot(qb, sf_ref[...].astype(BF16)) * xf_ref[...]
        oacc_ref[sl, :] = o
        sf_ref[...] = gcf * sf_ref[...] + _dot((k * zf_ref[...]).T.astype(BF16), vb)
        return carry

    def backward(cj, carry):
        ci = n_chunks - 1 - cj
        sl = pl.ds(pl.multiple_of(ci * c, c), c)
        k = ks_ref[sl, :]
        vb = v_ref[sl, :].astype(BF16)
        o = oacc_ref[sl, :]
        if use_cross:
            o = o + _dot(qs_ref[sl, :], sb_ref[...].astype(BF16)) * xb_ref[...]
        sb_ref[...] = gcb * sb_ref[...] + _dot((k * zb_ref[...]).T.astype(BF16), vb)
        o_ref[sl, :] = (_rms(o, gn_ref[...]) * _silu(g_ref[sl, :])).astype(BF16)
        return carry

    lax.fori_loop(0, n_chunks, forward, 0)
    lax.fori_loop(0, n_chunks, backward, 0)
    if not latent:
        so_ref[0] = sf_ref[...]
        so_ref[1] = sb_ref[...]


def _retention(z, coef, gn_g, batch, seq, n_heads, col0, layer, depth=None, cos=None, sin=None, state=None,
               prev_states=None):
    t = z.shape[0]
    dh = HEAD_DIM_C
    latent = state is not None
    n_chunks = seq // RET_CHUNK
    cb = col0 // dh
    zspec = lambda seg: pl.BlockSpec((seq, dh), lambda h, b: (b, cb + seg * n_heads + h))
    in_specs = [
        pl.BlockSpec(memory_space=pltpu.SMEM),
        zspec(0), zspec(1), zspec(2), zspec(3),
        pl.BlockSpec((1, dh), lambda h, b: (0, h)),
    ]
    args = [coef, z, z, z, z, gn_g.reshape(1, -1)]
    ospec = pl.BlockSpec((seq, dh), lambda h, b: (b, h))
    oshape = jax.ShapeDtypeStruct((t, n_heads * dh), BF16)
    state_spec = pl.BlockSpec((None, None, 2, None, dh, dh), lambda h, b: (b, layer, 0, h, 0, 0))
    aliases = {}
    if latent:
        tab = pl.BlockSpec((seq, dh), lambda h, b: (0, 0))
        in_specs += [tab, tab, state_spec]
        args += [cos, sin, state]
        out_specs, out_shape = ospec, oshape
    else:
        if prev_states is not None:
            aliases = {len(args): 1}
            args.append(prev_states)
            in_specs.append(pl.BlockSpec(memory_space=pl.ANY))
        out_specs = [ospec, state_spec]
        out_shape = [oshape, jax.ShapeDtypeStruct((batch, depth, 2, n_heads, dh, dh), F32)]
    sq = lambda: pltpu.VMEM((RET_CHUNK, RET_CHUNK), F32)
    kern = functools.partial(_ret_kernel, n_chunks=n_chunks, n_heads=n_heads, latent=latent)
    return pl.pallas_call(
        kern,
        grid=(n_heads, batch),
        in_specs=in_specs,
        out_specs=out_specs,
        out_shape=out_shape,
        input_output_aliases=aliases,
        scratch_shapes=[sq(), sq(), sq(), sq(), sq(), sq(), sq(),
                        pltpu.VMEM((seq, dh), F32), pltpu.VMEM((seq, dh), BF16), pltpu.VMEM((seq, dh), F32)],
        compiler_params=_cparams(2, 48),
        name="retention",
    )(*args)


def _out_proj_kernel(oa_ref, ob_ref, oc_ref, w_ref, x_ref, m_ref, y_ref, *, row0, tiles_per_cond):
    wa = oa_ref.shape[1]
    wb = ob_ref.shape[1]
    row = row0 + pl.program_id(0) // tiles_per_cond
    acc = _dot(oa_ref[...], w_ref[0:wa, :])
    acc = acc + _dot(ob_ref[...], w_ref[wa:wa + wb, :])
    acc = acc + _dot(oc_ref[...], w_ref[wa + wb:, :])
    y_ref[...] = x_ref[...] + m_ref[pl.ds(row, 1), :] * acc


def _out_proj(oa, ob, oc, w, x, mod, layer, row0, tokens_per_cond):
    t, d = x.shape
    tm, tn = 1024, 512
    gate_col0 = (2 * d) // tn
    kern = functools.partial(_out_proj_kernel, row0=row0, tiles_per_cond=tokens_per_cond // tm)
    act = lambda a: pl.BlockSpec((tm, a.shape[1]), lambda i, j: (i, 0))
    return pl.pallas_call(
        kern,
        grid=(t // tm, d // tn),
        in_specs=[
            act(oa), act(ob), act(oc),
            pl.BlockSpec((None, d, tn), lambda i, j: (layer, 0, j)),
            pl.BlockSpec((tm, tn), lambda i, j: (i, j)),
            pl.BlockSpec((None, COND_ROWS, tn), lambda i, j: (layer, 0, gate_col0 + j)),
        ],
        out_specs=pl.BlockSpec((tm, tn), lambda i, j: (i, j)),
        out_shape=jax.ShapeDtypeStruct((t, d), F32),
        compiler_params=_cparams(2, 48),
        name="out_proj",
    )(oa, ob, oc, w, x, mod)


def _rope_angles(pos, n_dims):
    half = n_dims // 2
    freqs = ROPE_BASE ** (-jnp.arange(half, dtype=F32) / half)
    return pos.astype(F32)[:, None] * freqs[None, :]


def _axial_tables(seq):
    t = jnp.arange(seq)
    ar = _rope_angles(t // GRID_W, HEAD_DIM_A // 2)
    ac = _rope_angles(t % GRID_W, HEAD_DIM_A // 2)
    cos = jnp.concatenate([jnp.cos(ar), jnp.cos(ar), jnp.cos(ac), jnp.cos(ac)], -1)
    sin = jnp.concatenate([-jnp.sin(ar), jnp.sin(ar), -jnp.sin(ac), jnp.sin(ac)], -1)
    return cos, sin


def _ret_tables(seq):
    ang = _rope_angles(jnp.arange(seq), HEAD_DIM_C)
    cos = jnp.concatenate([jnp.cos(ang), jnp.cos(ang)], -1)
    sin = jnp.concatenate([-jnp.sin(ang), jnp.sin(ang)], -1)
    return cos, sin


def kernel(x_prompt, x_sample, cache_k, cache_v, state_ret, c, c_ctx, norm_g, w_mod, b_mod, w_in, qn_g, kn_g,
           rpb, conv_w, conv_b, cln_g, cln_b, w_pw, ret_decay_logit, ret_dir_scale, ret_gn_g, w_out):
    batch, seq, d = x_prompt.shape
    dec_batch, dec_seq, _ = x_sample.shape
    depth = w_in.shape[0]
    n_heads_a = cache_k.shape[3]
    w_a = n_heads_a * HEAD_DIM_A
    w_b = w_pw.shape[1]
    n_heads_c = state_ret.shape[3]
    w_c = n_heads_c * HEAD_DIM_C
    col_b = 4 * w_a
    col_c = col_b + 3 * w_b
    assert w_in.shape[2] == col_c + 4 * w_c and w_a + w_b + w_c == d
    assert 1 + dec_batch <= COND_ROWS
    t_ctx = batch * seq
    t_lat = dec_batch * dec_seq
    rows = dec_seq // GRID_W
    kh = min(WIN_H, rows)

    cond = jnp.concatenate([c_ctx[None], c, jnp.zeros((COND_ROWS - 1 - dec_batch, d), F32)], axis=0)
    mod = _modulation(cond, w_mod, b_mod)
    w_in_b = w_in.astype(BF16)
    w_out_b = w_out.astype(BF16)
    w_pw_b = w_pw.astype(BF16)
    cache_k2 = cache_k.reshape(dec_batch, depth, -1, w_a)
    cache_v2 = cache_v.reshape(dec_batch, depth, -1, w_a)
    ax_cos, ax_sin = _axial_tables(dec_seq)
    rt_cos, rt_sin = _ret_tables(dec_seq)
    log_gamma = jax.nn.log_sigmoid(ret_decay_logit.astype(F32))
    ret_coef = jnp.concatenate([log_gamma, ret_dir_scale.astype(F32)], axis=1).reshape(depth, -1)

    xp = x_prompt.reshape(t_ctx, d)
    xs = x_sample.reshape(t_lat, d)
    caches = None
    states = None
    for l in range(depth):
        conv_args = (conv_w[l], conv_b[l], cln_g[l], cln_b[l], w_pw_b, l)
        h = _norm_mod(xp, norm_g[l], mod, l, 0, t_ctx)
        z = _in_proj(h, w_in_b, l)
        oa, *caches = _ctx_attention(z, qn_g[l], kn_g[l], batch, seq, w_a, depth, l, caches)
        ob = _conformer(z, *conv_args, seq, w_b, col_b)
        oc, states = _retention(z, ret_coef[l], ret_gn_g[l], batch, seq, n_heads_c, col_c, l, depth=depth,
                                prev_states=states)
        xp = _out_proj(oa, ob, oc, w_out_b, xp, mod, l, 0, t_ctx)
        h = _norm_mod(xs, norm_g[l], mod, l, 1, dec_seq)
        z = _in_proj(h, w_in_b, l)
        qr, kr, vb = _na_prep(z, ax_cos, ax_sin, qn_g[l], kn_g[l], dec_seq, w_a)
        bias = _rpb_table(rpb[l], kh)
        oa = _nbr_attention(qr, kr, vb, cache_k2, cache_v2, l, z, bias, dec_batch, dec_seq, w_a)
        ob = _conformer(z, *conv_args, dec_seq, w_b, col_b)
        oc = _retention(z, ret_coef[l], ret_gn_g[l], dec_batch, dec_seq, n_heads_c, col_c, l,
                        cos=rt_cos, sin=rt_sin, state=state_ret)
        xs = _out_proj(oa, ob, oc, w_out_b, xs, mod, l, 1, dec_seq)
    new_k, new_v = (a.reshape(batch, depth, seq, n_heads_a, HEAD_DIM_A) for a in caches)
    return (xp.reshape(batch, seq, d), xs.reshape(dec_batch, dec_seq, d), new_k, new_v, states)
```

```python
import functools

import jax
import jax.numpy as jnp
from jax import lax
from jax.experimental import pallas as pl
from jax.experimental.pallas import tpu as pltpu

F32 = jnp.float32
BF16 = jnp.bfloat16

GRID_W = 64
WIN_H = 8
WIN_W = 16
CONV_K = 31
HEAD_DIM_A = 128
HEAD_DIM_C = 256
ROPE_BASE = 10000.0
EPS = 1e-6
NEG_INF = -1e30

RET_CHUNK = 256
CONV_HALO = 16
COND_ROWS = 8
MIB = 1024 * 1024


def _cparams(n_axes, vmem_mib):
    return pltpu.CompilerParams(
        dimension_semantics=("arbitrary",) * n_axes,
        vmem_limit_bytes=vmem_mib * MIB,
    )


def _silu(x):
    return x * jax.nn.sigmoid(x)


def _dot(a, b):
    return jnp.dot(a, b, preferred_element_type=F32)


def _dot_nt(a, b):
    return lax.dot_general(a, b, (((1,), (1,)), ((), ())), preferred_element_type=F32)


def _rms(x, g):
    return x * lax.rsqrt(jnp.mean(x * x, axis=-1, keepdims=True) + EPS) * g


def _mod_kernel(cond_ref, w_ref, b_ref, o_ref):
    s = _silu(cond_ref[...]).astype(BF16)
    o_ref[0] = _dot(s, w_ref[0].astype(BF16)) + b_ref[0]


def _modulation(cond, w_mod, b_mod):
    depth, d, n = w_mod.shape
    tn = 512
    return pl.pallas_call(
        _mod_kernel,
        grid=(depth, n // tn),
        in_specs=[
            pl.BlockSpec((COND_ROWS, d), lambda l, j: (0, 0)),
            pl.BlockSpec((1, d, tn), lambda l, j: (l, 0, j)),
            pl.BlockSpec((1, 1, tn), lambda l, j: (l, 0, j)),
        ],
        out_specs=pl.BlockSpec((1, COND_ROWS, tn), lambda l, j: (l, 0, j)),
        out_shape=jax.ShapeDtypeStruct((depth, COND_ROWS, n), F32),
        compiler_params=_cparams(2, 40),
        name="modulation",
    )(cond, w_mod, b_mod.reshape(depth, 1, n))


def _norm_mod_kernel(x_ref, g_ref, m_ref, o_ref, *, row0, tiles_per_cond):
    d = x_ref.shape[1]
    row = row0 + pl.program_id(0) // tiles_per_cond
    shift = m_ref[pl.ds(row, 1), 0:d]
    scale = m_ref[pl.ds(row, 1), d:2 * d]
    y = _rms(x_ref[...], g_ref[...])
    o_ref[...] = (y * (1.0 + scale) + shift).astype(BF16)


def _norm_mod(x, g, mod, layer, row0, tokens_per_cond):
    t, d = x.shape
    tm = 512
    kern = functools.partial(_norm_mod_kernel, row0=row0, tiles_per_cond=tokens_per_cond // tm)
    return pl.pallas_call(
        kern,
        grid=(t // tm,),
        in_specs=[
            pl.BlockSpec((tm, d), lambda i: (i, 0)),
            pl.BlockSpec((1, d), lambda i: (0, 0)),
            pl.BlockSpec((None, COND_ROWS, 3 * d), lambda i: (layer, 0, 0)),
        ],
        out_specs=pl.BlockSpec((tm, d), lambda i: (i, 0)),
        out_shape=jax.ShapeDtypeStruct((t, d), BF16),
        compiler_params=_cparams(1, 40),
        name="norm_mod",
    )(x, g.reshape(1, d), mod)


def _matmul_kernel(a_ref, b_ref, o_ref):
    o_ref[...] = _dot(a_ref[...], b_ref[...])


def _in_proj(h, w, layer, col0):
    t, k = h.shape
    n = w.shape[2] - col0
    tm, tn = 1024, 1024
    cb0 = col0 // tn
    return pl.pallas_call(
        _matmul_kernel,
        grid=(t // tm, n // tn),
        in_specs=[
            pl.BlockSpec((tm, k), lambda i, j: (i, 0)),
            pl.BlockSpec((None, k, tn), lambda i, j: (layer, 0, cb0 + j)),
        ],
        out_specs=pl.BlockSpec((tm, tn), lambda i, j: (i, j)),
        out_shape=jax.ShapeDtypeStruct((t, n), F32),
        compiler_params=_cparams(2, 52),
        name="in_proj",
    )(h, w)


def _ctx_attn_kernel(q_ref, k_ref, v_ref, g_ref, o_ref):
    n_heads = q_ref.shape[1] // HEAD_DIM_A
    scale = HEAD_DIM_A ** -0.5
    for h in range(n_heads):
        hs = slice(h * HEAD_DIM_A, (h + 1) * HEAD_DIM_A)
        s = _dot_nt(q_ref[:, hs], k_ref[:, hs].astype(BF16)) * scale
        p = jnp.exp(s - jnp.max(s, axis=-1, keepdims=True))
        o = _dot(p.astype(BF16), v_ref[:, hs].astype(BF16)) / jnp.sum(p, axis=-1, keepdims=True)
        o_ref[:, hs] = (o * _silu(g_ref[:, hs])).astype(BF16)


def _ctx_attention(q, cache_k, cache_v, layer, z, batch, seq, w_a):
    t = q.shape[0]
    cache_spec = pl.BlockSpec((None, None, seq, w_a), lambda b: (b, layer, 0, 0))
    tok_spec = pl.BlockSpec((seq, w_a), lambda b: (b, 0))
    return pl.pallas_call(
        _ctx_attn_kernel,
        grid=(batch,),
        in_specs=[tok_spec, cache_spec, cache_spec, tok_spec],
        out_specs=tok_spec,
        out_shape=jax.ShapeDtypeStruct((t, w_a), BF16),
        compiler_params=_cparams(1, 40),
        name="ctx_attention",
    )(q, cache_k, cache_v, z)


def _swap_halves_32(x):
    lane = lax.broadcasted_iota(jnp.int32, x.shape, 1)
    return jnp.where((lane & 32) == 0, pltpu.roll(x, 96, 1), pltpu.roll(x, 32, 1))


def _qkv_kernel(a_ref, wq_ref, wk_ref, wv_ref, qg_ref, kg_ref, *rest, latent, chunk):
    q_out, k_out, v_out = rest[-3:]
    n_heads = wq_ref.shape[1] // HEAD_DIM_A
    for c in range(a_ref.shape[0] // chunk):
        rs = slice(c * chunk, (c + 1) * chunk)
        a = a_ref[rs, :]
        q = _dot(a, wq_ref[...])
        k = _dot(a, wk_ref[...])
        v = _dot(a, wv_ref[...])
        for h in range(n_heads):
            hs = slice(h * HEAD_DIM_A, (h + 1) * HEAD_DIM_A)
            qn = _rms(q[:, hs], qg_ref[...])
            kn = _rms(k[:, hs], kg_ref[...])
            if latent:
                cos = rest[0][rs, :]
                sin = rest[1][rs, :]
                q_out[rs, hs] = (qn * cos + _swap_halves_32(qn) * sin).astype(BF16)
                k_out[rs, hs] = (kn * cos + _swap_halves_32(kn) * sin).astype(BF16)
            else:
                q_out[rs, hs] = qn.astype(BF16)
                k_out[c, :, hs] = kn
        if latent:
            v_out[rs, :] = v.astype(BF16)
        else:
            v_out[c] = v


def _qkv_proj(h, w, layer, qg, kg, w_a, seq, cos=None, sin=None, batch=None, depth=None, caches=None):
    t, kdim = h.shape
    latent = cos is not None
    tm, tn = 1024, 256
    nb = w_a // tn
    wspec = lambda seg: pl.BlockSpec((None, kdim, tn), lambda i, j: (layer, 0, seg * nb + j))
    gspec = pl.BlockSpec((1, HEAD_DIM_A), lambda i, j: (0, 0))
    tok_spec = pl.BlockSpec((tm, tn), lambda i, j: (i, j))
    tok_shape = jax.ShapeDtypeStruct((t, w_a), BF16)
    args = [h, w, w, w, qg.reshape(1, -1), kg.reshape(1, -1)]
    in_specs = [pl.BlockSpec((tm, kdim), lambda i, j: (i, 0)), wspec(0), wspec(1), wspec(2), gspec, gspec]
    aliases = {}
    if latent:
        tab = pl.BlockSpec((tm, HEAD_DIM_A), lambda i, j: (i % (seq // tm), 0))
        args += [cos, sin]
        in_specs += [tab, tab]
        out_specs = [tok_spec] * 3
        out_shape = [tok_shape] * 3
    else:
        cache_spec = pl.BlockSpec((tm // seq, None, seq, tn), lambda i, j: (i, layer, 0, j))
        cache_shape = jax.ShapeDtypeStruct((batch, depth, seq, w_a), F32)
        if caches is not None:
            aliases = {len(args): 1, len(args) + 1: 2}
            args += list(caches)
            in_specs += [pl.BlockSpec(memory_space=pl.ANY)] * 2
        out_specs = [tok_spec, cache_spec, cache_spec]
        out_shape = [tok_shape, cache_shape, cache_shape]
    return pl.pallas_call(
        functools.partial(_qkv_kernel, latent=latent, chunk=256 if latent else seq),
        grid=(t // tm, nb),
        in_specs=in_specs,
        out_specs=out_specs,
        out_shape=out_shape,
        input_output_aliases=aliases,
        compiler_params=_cparams(2, 48),
        name="qkv_proj",
    )(*args)


def _rpb_table_kernel(rpb_ref, o_ref):
    n = o_ref.shape[1]
    lane = lax.broadcasted_iota(jnp.int32, (1, n), 1)
    qcol = lane >> (GRID_W.bit_length() - 1)
    kcol = lane & (GRID_W - 1)
    dc = jnp.clip(kcol - qcol, 1 - WIN_W, WIN_W - 1) + (WIN_W - 1)
    c0 = jnp.clip(qcol - WIN_W // 2, 0, GRID_W - WIN_W)
    in_win = (kcol >= c0) & (kcol < c0 + WIN_W)
    r = rpb_ref[...]
    acc = jnp.zeros(o_ref.shape, F32)
    for j in range(2 * WIN_W - 1):
        acc = jnp.where(dc == j, r[:, j:j + 1], acc)
    o_ref[...] = jnp.where(in_win, acc, NEG_INF)


def _rpb_table(rpb_l, kh):
    n_heads, n_dr, n_dc = rpb_l.shape
    flat = pl.pallas_call(
        _rpb_table_kernel,
        out_shape=jax.ShapeDtypeStruct((n_heads * n_dr, GRID_W * GRID_W), F32),
        name="rpb_table",
    )(rpb_l.reshape(n_heads * n_dr, n_dc))
    b = flat.reshape(n_heads, n_dr, GRID_W, GRID_W)
    slabs = jnp.stack([b[:, d:d + kh] for d in range(WIN_H)], axis=0)
    return slabs.transpose(0, 1, 3, 2, 4).reshape(WIN_H, n_heads, GRID_W, kh * GRID_W)


NA_QROWS = 4
NA_KROWS = NA_QROWS + WIN_H


def _na_block_plan(rows, kh):
    plans = {}
    for jb in range(rows // NA_QROWS):
        k0 = min(max(NA_QROWS * jb - kh // 2, 0), rows - NA_KROWS)
        plan = []
        for i in range(NA_QROWS):
            r = NA_QROWS * jb + i
            r0 = min(max(r - kh // 2, 0), rows - kh)
            assert 0 <= r0 - k0 <= NA_KROWS - kh
            plan.append((r0 - k0, r0 - r + WIN_H - 1))
        plans.setdefault(tuple(plan), []).append(jb)
    return plans


def _na_kernel(q_ref, k_ref, v_ref, kc_ref, vc_ref, g_ref, t_ref, ts_ref, o_ref, bias_ref, kcb_ref, vcb_ref,
               *, rows, kh, heads_per_step, shifted_dr):
    scale = HEAD_DIM_A ** -0.5
    jb = pl.program_id(1)
    span = kh * GRID_W

    @pl.when(pl.program_id(2) == 0)
    def _assemble():
        for plan, jbs in _na_block_plan(rows, kh).items():
            pred = functools.reduce(jnp.logical_or, [jb == j for j in jbs])

            @pl.when(pred)
            def _():
                for hh in range(heads_per_step):
                    for i, (off, dr0) in enumerate(plan):
                        rs = slice(i * GRID_W, (i + 1) * GRID_W)
                        bias_ref[hh, rs, :] = jnp.full((GRID_W, NA_KROWS * GRID_W), NEG_INF, F32)
                        if off % 2 == 0:
                            bias_ref[hh, rs, off * GRID_W:off * GRID_W + span] = t_ref[dr0, hh]
                        else:
                            assert dr0 == shifted_dr
                            lo = (off - 1) * GRID_W
                            bias_ref[hh, rs, lo:lo + span + 2 * GRID_W] = ts_ref[hh]

    kcb_ref[...] = kc_ref[...].astype(BF16)
    vcb_ref[...] = vc_ref[...].astype(BF16)
    k0 = jnp.clip(NA_QROWS * jb - kh // 2, 0, rows - NA_KROWS)
    ks = pl.ds(pl.multiple_of(k0 * GRID_W, NA_QROWS * GRID_W), NA_KROWS * GRID_W)
    for hh in range(heads_per_step):
        hs = slice(hh * HEAD_DIM_A, (hh + 1) * HEAD_DIM_A)
        q = q_ref[:, hs]
        s_loc = _dot_nt(q, k_ref[ks, hs]) * scale + bias_ref[hh]
        s_ctx = _dot_nt(q, kcb_ref[:, hs]) * scale
        m = jnp.maximum(jnp.max(s_loc, axis=-1, keepdims=True), jnp.max(s_ctx, axis=-1, keepdims=True))
        p_loc = jnp.exp(s_loc - m)
        p_ctx = jnp.exp(s_ctx - m)
        den = jnp.sum(p_loc, axis=-1, keepdims=True) + jnp.sum(p_ctx, axis=-1, keepdims=True)
        o = (_dot(p_loc.astype(BF16), v_ref[ks, hs]) + _dot(p_ctx.astype(BF16), vcb_ref[:, hs])) / den
        o_ref[:, hs] = (o * _silu(g_ref[:, hs])).astype(BF16)


def _nbr_attention(qr, kr, vb, cache_k, cache_v, layer, z, bias, batch, seq, w_a):
    t = qr.shape[0]
    rows = seq // GRID_W
    kh = min(WIN_H, rows)
    assert rows % NA_QROWS == 0 and rows >= NA_KROWS and NA_QROWS % 2 == 0
    heads_per_step = 4
    wb = heads_per_step * HEAD_DIM_A
    tq = NA_QROWS * GRID_W
    steps_per_seq = seq // tq
    lc = cache_k.shape[2]
    ga_col0 = 0
    shifted_dr = WIN_H - 1 - kh // 2
    pad = ((0, 0), (0, 0), (GRID_W, GRID_W))
    bias_shifted = jnp.pad(bias[shifted_dr], pad, constant_values=NEG_INF)
    kern = functools.partial(_na_kernel, rows=rows, kh=kh, heads_per_step=heads_per_step, shifted_dr=shifted_dr)
    qspec = pl.BlockSpec((tq, wb), lambda g, r, b: (b * steps_per_seq + r, g))
    kvspec = pl.BlockSpec((seq, wb), lambda g, r, b: (b, g))
    cspec = pl.BlockSpec((None, None, lc, wb), lambda g, r, b: (b, layer, 0, g))
    return pl.pallas_call(
        kern,
        grid=(w_a // wb, steps_per_seq, batch),
        in_specs=[
            qspec, kvspec, kvspec, cspec, cspec,
            pl.BlockSpec((tq, wb), lambda g, r, b: (b * steps_per_seq + r, ga_col0 + g)),
            pl.BlockSpec((WIN_H, heads_per_step, GRID_W, kh * GRID_W), lambda g, r, b: (0, g, 0, 0)),
            pl.BlockSpec((heads_per_step, GRID_W, (kh + 2) * GRID_W), lambda g, r, b: (g, 0, 0)),
        ],
        out_specs=qspec,
        out_shape=jax.ShapeDtypeStruct((t, w_a), BF16),
        scratch_shapes=[pltpu.VMEM((heads_per_step, tq, NA_KROWS * GRID_W), F32),
                        pltpu.VMEM((lc, wb), BF16), pltpu.VMEM((lc, wb), BF16)],
        compiler_params=_cparams(3, 48),
        name="nbr_attention",
    )(qr, kr, vb, cache_k, cache_v, z, bias, bias_shifted)


def _conv_kernel(u_ref, gl_ref, up_ref, glp_ref, un_ref, gln_ref, gb_ref, cw_ref, cb_ref, lng_ref, lnb_ref,
                 wpw_ref, o_ref, hp_ref, hs_ref, acc_ref, *, tiles_per_seq):
    tl, wb = u_ref.shape
    i = pl.program_id(0)
    first = (i % tiles_per_seq) == 0
    last = (i % tiles_per_seq) == tiles_per_seq - 1
    halo = CONV_HALO
    hp_ref[0:halo, :] = jnp.where(first, 0.0, up_ref[...] * jax.nn.sigmoid(glp_ref[...]))
    hp_ref[halo:halo + tl, :] = u_ref[...] * jax.nn.sigmoid(gl_ref[...])
    hp_ref[halo + tl:2 * halo + tl, :] = jnp.where(last, 0.0, un_ref[...] * jax.nn.sigmoid(gln_ref[...]))
    span = hs_ref.shape[1]
    for s in range(8):
        hs_ref[s] = hp_ref[s:s + span, :]

    rt = 128
    lanes = 128
    off = halo - CONV_K // 2

    for cc in range(wb // lanes):
        cs = slice(cc * lanes, (cc + 1) * lanes)

        def row_body(ri, carry, cs=cs):
            r0 = pl.multiple_of(ri * rt, rt)
            acc = jnp.broadcast_to(cb_ref[:, cs], (rt, lanes))
            for s in range(8):
                taps = [k for k in range(CONV_K) if (k + off) % 8 == s]
                reach = 8 * max((k + off) // 8 for k in taps)
                win = hs_ref[s, pl.ds(r0, rt + reach), cs]
                for k in taps:
                    a = (k + off) // 8
                    acc = acc + win[8 * a:8 * a + rt] * cw_ref[k:k + 1, cs]
            acc_ref[pl.ds(r0, rt), cs] = acc
            return carry

        lax.fori_loop(0, tl // rt, row_body, 0)

    y = acc_ref[...]
    mu = jnp.mean(y, axis=-1, keepdims=True)
    yc = y - mu
    var = jnp.mean(yc * yc, axis=-1, keepdims=True)
    yn = yc * lax.rsqrt(var + EPS) * lng_ref[...] + lnb_ref[...]
    ob = _dot(_silu(yn).astype(BF16), wpw_ref[...])
    o_ref[...] = (ob * _silu(gb_ref[...])).astype(BF16)


def _conformer(z, conv_w, conv_b, ln_g, ln_b, w_pw, layer, seq, w_b, col0):
    t = z.shape[0]
    tl = min(seq, 256)
    tiles_per_seq = seq // tl
    hb = tl // CONV_HALO
    n_hb = t // CONV_HALO
    c_u, c_gl, c_gb = col0 // w_b, col0 // w_b + 1, col0 // w_b + 2
    main = lambda c: pl.BlockSpec((tl, w_b), lambda i: (i, c))
    prev = lambda c: pl.BlockSpec((CONV_HALO, w_b), lambda i: (jnp.maximum(i * hb - 1, 0), c))
    nxt = lambda c: pl.BlockSpec((CONV_HALO, w_b), lambda i: (jnp.minimum((i + 1) * hb, n_hb - 1), c))
    row = pl.BlockSpec((1, w_b), lambda i: (0, 0))
    span = tl + 2 * CONV_HALO - 8
    kern = functools.partial(_conv_kernel, tiles_per_seq=tiles_per_seq)
    return pl.pallas_call(
        kern,
        grid=(t // tl,),
        in_specs=[
            main(c_u), main(c_gl), prev(c_u), prev(c_gl), nxt(c_u), nxt(c_gl), main(c_gb),
            pl.BlockSpec((CONV_K, w_b), lambda i: (0, 0)), row, row, row,
            pl.BlockSpec((None, w_b, w_b), lambda i: (layer, 0, 0)),
        ],
        out_specs=pl.BlockSpec((tl, w_b), lambda i: (i, 0)),
        out_shape=jax.ShapeDtypeStruct((t, w_b), BF16),
        scratch_shapes=[
            pltpu.VMEM((tl + 2 * CONV_HALO, w_b), F32),
            pltpu.VMEM((8, span, w_b), F32),
            pltpu.VMEM((tl, w_b), F32),
        ],
        compiler_params=_cparams(1, 48),
        name="conformer",
    )(z, z, z, z, z, z, z, conv_w, conv_b.reshape(1, -1), ln_g.reshape(1, -1), ln_b.reshape(1, -1), w_pw)


def _swap_halves_128(x):
    half = x.shape[1] // 2
    return jnp.concatenate([x[:, half:], x[:, :half]], axis=1)


def _ret_kernel(coef_ref, q_ref, k_ref, v_ref, g_ref, gn_ref, *rest, n_chunks, n_heads, latent):
    n_scratch = 10
    scratch = rest[-n_scratch:]
    if latent:
        cos_ref, sin_ref, s0_ref, o_ref = rest[:4]
    else:
        o_ref, so_ref = rest[-n_scratch - 2:-n_scratch]
    m_ref, xf_ref, xb_ref, zf_ref, zb_ref, sf_ref, sb_ref, oacc_ref, qs_ref, ks_ref = scratch
    c = RET_CHUNK
    h = pl.program_id(0)
    lgf = coef_ref[h]
    lgb = coef_ref[n_heads + h]
    dsf = coef_ref[2 * n_heads + h]
    dsb = coef_ref[3 * n_heads + h]

    @pl.when(pl.program_id(1) == 0)
    def _tables():
        rows = lax.broadcasted_iota(jnp.int32, (c, c), 0).astype(F32)
        cols = lax.broadcasted_iota(jnp.int32, (c, c), 1).astype(F32)
        diff = rows - cols
        m_ref[...] = (jnp.where(diff >= 0, jnp.exp(lgf * jnp.maximum(diff, 0.0)), 0.0) * dsf
                      + jnp.where(diff <= 0, jnp.exp(lgb * jnp.maximum(-diff, 0.0)), 0.0) * dsb)
        xf_ref[...] = dsf * jnp.exp(lgf * (rows + 1.0))
        xb_ref[...] = dsb * jnp.exp(lgb * (c - rows))
        zf_ref[...] = jnp.exp(lgf * (c - 1.0 - rows))
        zb_ref[...] = jnp.exp(lgb * rows)

    chunk_len = jnp.full((1, c), float(c), F32)
    gcf = jnp.exp(lgf * chunk_len)
    gcb = jnp.exp(lgb * chunk_len)

    k_scale = HEAD_DIM_C ** -0.5
    if not latent and n_chunks == 1:
        for si in range(q_ref.shape[0] // c):
            rs = slice(si * c, (si + 1) * c)
            k = k_ref[rs, :] * k_scale
            vb = v_ref[rs, :].astype(BF16)
            a = _dot_nt(q_ref[rs, :].astype(BF16), k.astype(BF16))
            o = _dot((a * m_ref[...]).astype(BF16), vb)
            so_ref[si, 0] = _dot((k * zf_ref[...]).T.astype(BF16), vb)
            so_ref[si, 1] = _dot((k * zb_ref[...]).T.astype(BF16), vb)
            o_ref[rs, :] = (_rms(o, gn_ref[...]) * _silu(g_ref[rs, :])).astype(BF16)
        return

    if latent:
        sf_ref[...] = s0_ref[0]
        sb_ref[...] = s0_ref[1]
    else:
        sf_ref[...] = jnp.zeros((c, c), F32)
        sb_ref[...] = jnp.zeros((c, c), F32)
    use_cross = latent or n_chunks > 1

    def forward(ci, carry):
        sl = pl.ds(pl.multiple_of(ci * c, c), c)
        q = q_ref[sl, :]
        k = k_ref[sl, :] * k_scale
        if latent:
            cos = cos_ref[sl, :]
            sin = sin_ref[sl, :]
            q = q * cos + _swap_halves_128(q) * sin
            k = k * cos + _swap_halves_128(k) * sin
        qb = q.astype(BF16)
        vb = v_ref[sl, :].astype(BF16)
        qs_ref[sl, :] = qb
        ks_ref[sl, :] = k
        a = _dot_nt(qb, k.astype(BF16))
        o = _dot((a * m_ref[...]).astype(BF16), vb)
        if use_cross:
            o = o + _dot(qb, sf_ref[...].astype(BF16)) * xf_ref[...]
        oacc_ref[sl, :] = o
        sf_ref[...] = gcf * sf_ref[...] + _dot((k * zf_ref[...]).T.astype(BF16), vb)
        return carry

    def backward(cj, carry):
        ci = n_chunks - 1 - cj
        sl = pl.ds(pl.multiple_of(ci * c, c), c)
        k = ks_ref[sl, :]
        vb = v_ref[sl, :].astype(BF16)
        o = oacc_ref[sl, :]
        if use_cross:
            o = o + _dot(qs_ref[sl, :], sb_ref[...].astype(BF16)) * xb_ref[...]
        sb_ref[...] = gcb * sb_ref[...] + _dot((k * zb_ref[...]).T.astype(BF16), vb)
        o_ref[sl, :] = (_rms(o, gn_ref[...]) * _silu(g_ref[sl, :])).astype(BF16)
        return carry

    lax.fori_loop(0, n_chunks, forward, 0)
    lax.fori_loop(0, n_chunks, backward, 0)
    if not latent:
        so_ref[0, 0] = sf_ref[...]
        so_ref[0, 1] = sb_ref[...]


def _retention(z, coef, gn_g, batch, seq, n_heads, col0, layer, depth=None, cos=None, sin=None, state=None,
               prev_states=None):
    t = z.shape[0]
    dh = HEAD_DIM_C
    latent = state is not None
    n_chunks = seq // RET_CHUNK
    cb = col0 // dh
    seqs = 4 if (not latent and n_chunks == 1 and batch % 4 == 0) else 1
    rows = seqs * seq
    zspec = lambda seg: pl.BlockSpec((rows, dh), lambda h, b: (b, cb + seg * n_heads + h))
    in_specs = [
        pl.BlockSpec(memory_space=pltpu.SMEM),
        zspec(0), zspec(1), zspec(2), zspec(3),
        pl.BlockSpec((1, dh), lambda h, b: (0, h)),
    ]
    args = [coef, z, z, z, z, gn_g.reshape(1, -1)]
    ospec = pl.BlockSpec((rows, dh), lambda h, b: (b, h))
    oshape = jax.ShapeDtypeStruct((t, n_heads * dh), BF16)
    aliases = {}
    if latent:
        tab = pl.BlockSpec((seq, dh), lambda h, b: (0, 0))
        state_spec = pl.BlockSpec((None, None, 2, None, dh, dh), lambda h, b: (b, layer, 0, h, 0, 0))
        in_specs += [tab, tab, state_spec]
        args += [cos, sin, state]
        out_specs, out_shape = ospec, oshape
    else:
        if prev_states is not None:
            aliases = {len(args): 1}
            args.append(prev_states)
            in_specs.append(pl.BlockSpec(memory_space=pl.ANY))
        state_spec = pl.BlockSpec((seqs, None, 2, None, dh, dh), lambda h, b: (b, layer, 0, h, 0, 0))
        out_specs = [ospec, state_spec]
        out_shape = [oshape, jax.ShapeDtypeStruct((batch, depth, 2, n_heads, dh, dh), F32)]
    sq = lambda: pltpu.VMEM((RET_CHUNK, RET_CHUNK), F32)
    kern = functools.partial(_ret_kernel, n_chunks=n_chunks, n_heads=n_heads, latent=latent)
    return pl.pallas_call(
        kern,
        grid=(n_heads, batch // seqs),
        in_specs=in_specs,
        out_specs=out_specs,
        out_shape=out_shape,
        input_output_aliases=aliases,
        scratch_shapes=[sq(), sq(), sq(), sq(), sq(), sq(), sq(),
                        pltpu.VMEM((seq, dh), F32), pltpu.VMEM((seq, dh), BF16), pltpu.VMEM((seq, dh), F32)],
        compiler_params=_cparams(2, 48),
        name="retention",
    )(*args)


def _out_proj_kernel(oa_ref, ob_ref, oc_ref, w_ref, x_ref, m_ref, y_ref, *, row0, tiles_per_cond):
    wa = oa_ref.shape[1]
    wb = ob_ref.shape[1]
    row = row0 + pl.program_id(0) // tiles_per_cond
    acc = _dot(oa_ref[...], w_ref[0:wa, :])
    acc = acc + _dot(ob_ref[...], w_ref[wa:wa + wb, :])
    acc = acc + _dot(oc_ref[...], w_ref[wa + wb:, :])
    y_ref[...] = x_ref[...] + m_ref[pl.ds(row, 1), :] * acc


def _out_proj(oa, ob, oc, w, x, mod, layer, row0, tokens_per_cond):
    t, d = x.shape
    tm, tn = 1024, 1024
    gate_col0 = (2 * d) // tn
    kern = functools.partial(_out_proj_kernel, row0=row0, tiles_per_cond=tokens_per_cond // tm)
    act = lambda a: pl.BlockSpec((tm, a.shape[1]), lambda i, j: (i, 0))
    return pl.pallas_call(
        kern,
        grid=(t // tm, d // tn),
        in_specs=[
            act(oa), act(ob), act(oc),
            pl.BlockSpec((None, d, tn), lambda i, j: (layer, 0, j)),
            pl.BlockSpec((tm, tn), lambda i, j: (i, j)),
            pl.BlockSpec((None, COND_ROWS, tn), lambda i, j: (layer, 0, gate_col0 + j)),
        ],
        out_specs=pl.BlockSpec((tm, tn), lambda i, j: (i, j)),
        out_shape=jax.ShapeDtypeStruct((t, d), F32),
        compiler_params=_cparams(2, 56),
        name="out_proj",
    )(oa, ob, oc, w, x, mod)


def _rope_angles(pos, n_dims):
    half = n_dims // 2
    freqs = ROPE_BASE ** (-jnp.arange(half, dtype=F32) / half)
    return pos.astype(F32)[:, None] * freqs[None, :]


def _axial_tables(seq):
    t = jnp.arange(seq)
    ar = _rope_angles(t // GRID_W, HEAD_DIM_A // 2)
    ac = _rope_angles(t % GRID_W, HEAD_DIM_A // 2)
    cos = jnp.concatenate([jnp.cos(ar), jnp.cos(ar), jnp.cos(ac), jnp.cos(ac)], -1)
    sin = jnp.concatenate([-jnp.sin(ar), jnp.sin(ar), -jnp.sin(ac), jnp.sin(ac)], -1)
    return cos, sin


def _ret_tables(seq):
    ang = _rope_angles(jnp.arange(seq), HEAD_DIM_C)
    cos = jnp.concatenate([jnp.cos(ang), jnp.cos(ang)], -1)
    sin = jnp.concatenate([-jnp.sin(ang), jnp.sin(ang)], -1)
    return cos, sin


def kernel(x_prompt, x_sample, cache_k, cache_v, state_ret, c, c_ctx, norm_g, w_mod, b_mod, w_in, qn_g, kn_g,
           rpb, conv_w, conv_b, cln_g, cln_b, w_pw, ret_decay_logit, ret_dir_scale, ret_gn_g, w_out):
    batch, seq, d = x_prompt.shape
    dec_batch, dec_seq, _ = x_sample.shape
    depth = w_in.shape[0]
    n_heads_a = cache_k.shape[3]
    w_a = n_heads_a * HEAD_DIM_A
    w_b = w_pw.shape[1]
    n_heads_c = state_ret.shape[3]
    w_c = n_heads_c * HEAD_DIM_C
    col_b = w_a
    col_c = col_b + 3 * w_b
    assert w_in.shape[2] == 3 * w_a + col_c + 4 * w_c and w_a + w_b + w_c == d
    assert 1 + dec_batch <= COND_ROWS
    t_ctx = batch * seq
    t_lat = dec_batch * dec_seq
    rows = dec_seq // GRID_W
    kh = min(WIN_H, rows)

    cond = jnp.concatenate([c_ctx[None], c, jnp.zeros((COND_ROWS - 1 - dec_batch, d), F32)], axis=0)
    mod = _modulation(cond, w_mod, b_mod)
    w_in_b = w_in.astype(BF16)
    w_out_b = w_out.astype(BF16)
    w_pw_b = w_pw.astype(BF16)
    cache_k2 = cache_k.reshape(dec_batch, depth, -1, w_a)
    cache_v2 = cache_v.reshape(dec_batch, depth, -1, w_a)
    ax_cos, ax_sin = _axial_tables(dec_seq)
    rt_cos, rt_sin = _ret_tables(dec_seq)
    log_gamma = jax.nn.log_sigmoid(ret_decay_logit.astype(F32))
    ret_coef = jnp.concatenate([log_gamma, ret_dir_scale.astype(F32)], axis=1).reshape(depth, -1)

    xp = x_prompt.reshape(t_ctx, d)
    xs = x_sample.reshape(t_lat, d)
    caches = None
    states = None
    for l in range(depth):
        conv_args = (conv_w[l], conv_b[l], cln_g[l], cln_b[l], w_pw_b, l)
        h = _norm_mod(xp, norm_g[l], mod, l, 0, t_ctx)
        q, *caches = _qkv_proj(h, w_in_b, l, qn_g[l], kn_g[l], w_a, seq, batch=batch, depth=depth, caches=caches)
        z = _in_proj(h, w_in_b, l, 3 * w_a)
        oa = _ctx_attention(q, *caches, l, z, batch, seq, w_a)
        ob = _conformer(z, *conv_args, seq, w_b, col_b)
        oc, states = _retention(z, ret_coef[l], ret_gn_g[l], batch, seq, n_heads_c, col_c, l, depth=depth,
                                prev_states=states)
        xp = _out_proj(oa, ob, oc, w_out_b, xp, mod, l, 0, t_ctx)
        h = _norm_mod(xs, norm_g[l], mod, l, 1, dec_seq)
        qr, kr, vb = _qkv_proj(h, w_in_b, l, qn_g[l], kn_g[l], w_a, dec_seq, cos=ax_cos, sin=ax_sin)
        z = _in_proj(h, w_in_b, l, 3 * w_a)
        bias = _rpb_table(rpb[l], kh)
        oa = _nbr_attention(qr, kr, vb, cache_k2, cache_v2, l, z, bias, dec_batch, dec_seq, w_a)
        ob = _conformer(z, *conv_args, dec_seq, w_b, col_b)
        oc = _retention(z, ret_coef[l], ret_gn_g[l], dec_batch, dec_seq, n_heads_c, col_c, l,
                        cos=rt_cos, sin=rt_sin, state=state_ret)
        xs = _out_proj(oa, ob, oc, w_out_b, xs, mod, l, 1, dec_seq)
    new_k, new_v = (a.reshape(batch, depth, seq, n_heads_a, HEAD_DIM_A) for a in caches)
    return (xp.reshape(batch, seq, d), xs.reshape(dec_batch, dec_seq, d), new_k, new_v, states)
```

```python
import functools

import jax
import jax.numpy as jnp
from jax import lax
from jax.experimental import pallas as pl
from jax.experimental.pallas import tpu as pltpu

F32 = jnp.float32
BF16 = jnp.bfloat16

GRID_W = 64
WIN_H = 8
WIN_W = 16
CONV_K = 31
HEAD_DIM_A = 128
HEAD_DIM_C = 256
ROPE_BASE = 10000.0
EPS = 1e-6
NEG_INF = -1e30

RET_CHUNK = 256
CONV_HALO = 16
COND_ROWS = 8
MIB = 1024 * 1024


def _cparams(n_axes, vmem_mib):
    return pltpu.CompilerParams(
        dimension_semantics=("arbitrary",) * n_axes,
        vmem_limit_bytes=vmem_mib * MIB,
    )


def _silu(x):
    return x * jax.nn.sigmoid(x)


def _dot(a, b):
    return jnp.dot(a, b, preferred_element_type=F32)


def _dot_nt(a, b):
    return lax.dot_general(a, b, (((1,), (1,)), ((), ())), preferred_element_type=F32)


def _rms(x, g):
    return x * lax.rsqrt(jnp.mean(x * x, axis=-1, keepdims=True) + EPS) * g


def _mod_kernel(cond_ref, w_ref, b_ref, o_ref):
    s = _silu(cond_ref[...]).astype(BF16)
    o_ref[0] = _dot(s, w_ref[0].astype(BF16)) + b_ref[0]


def _modulation(cond, w_mod, b_mod):
    depth, d, n = w_mod.shape
    tn = 512
    return pl.pallas_call(
        _mod_kernel,
        grid=(depth, n // tn),
        in_specs=[
            pl.BlockSpec((COND_ROWS, d), lambda l, j: (0, 0)),
            pl.BlockSpec((1, d, tn), lambda l, j: (l, 0, j)),
            pl.BlockSpec((1, 1, tn), lambda l, j: (l, 0, j)),
        ],
        out_specs=pl.BlockSpec((1, COND_ROWS, tn), lambda l, j: (l, 0, j)),
        out_shape=jax.ShapeDtypeStruct((depth, COND_ROWS, n), F32),
        compiler_params=_cparams(2, 40),
        name="modulation",
    )(cond, w_mod, b_mod.reshape(depth, 1, n))


def _norm_mod_kernel(x_ref, g_ref, m_ref, o_ref, *, row0, tiles_per_cond):
    d = x_ref.shape[1]
    row = row0 + pl.program_id(0) // tiles_per_cond
    shift = m_ref[pl.ds(row, 1), 0:d]
    scale = m_ref[pl.ds(row, 1), d:2 * d]
    y = _rms(x_ref[...], g_ref[...])
    o_ref[...] = (y * (1.0 + scale) + shift).astype(BF16)


def _norm_mod(x, g, mod, layer, row0, tokens_per_cond):
    t, d = x.shape
    tm = 512
    kern = functools.partial(_norm_mod_kernel, row0=row0, tiles_per_cond=tokens_per_cond // tm)
    return pl.pallas_call(
        kern,
        grid=(t // tm,),
        in_specs=[
            pl.BlockSpec((tm, d), lambda i: (i, 0)),
            pl.BlockSpec((1, d), lambda i: (0, 0)),
            pl.BlockSpec((None, COND_ROWS, 3 * d), lambda i: (layer, 0, 0)),
        ],
        out_specs=pl.BlockSpec((tm, d), lambda i: (i, 0)),
        out_shape=jax.ShapeDtypeStruct((t, d), BF16),
        compiler_params=_cparams(1, 40),
        name="norm_mod",
    )(x, g.reshape(1, d), mod)


def _matmul_kernel(a_ref, b_ref, o_ref):
    o_ref[...] = _dot(a_ref[...], b_ref[...])


def _in_proj(h, w, layer, col0):
    t, k = h.shape
    n = w.shape[2] - col0
    tm, tn = 1024, 1024
    cb0 = col0 // tn
    return pl.pallas_call(
        _matmul_kernel,
        grid=(t // tm, n // tn),
        in_specs=[
            pl.BlockSpec((tm, k), lambda i, j: (i, 0)),
            pl.BlockSpec((None, k, tn), lambda i, j: (layer, 0, cb0 + j)),
        ],
        out_specs=pl.BlockSpec((tm, tn), lambda i, j: (i, j)),
        out_shape=jax.ShapeDtypeStruct((t, n), F32),
        compiler_params=_cparams(2, 52),
        name="in_proj",
    )(h, w)


def _ctx_attn_kernel(q_ref, k_ref, v_ref, g_ref, o_ref):
    n_heads = q_ref.shape[1] // HEAD_DIM_A
    scale = HEAD_DIM_A ** -0.5
    for h in range(n_heads):
        hs = slice(h * HEAD_DIM_A, (h + 1) * HEAD_DIM_A)
        s = _dot_nt(q_ref[:, hs], k_ref[:, hs].astype(BF16)) * scale
        p = jnp.exp(s - jnp.max(s, axis=-1, keepdims=True))
        o = _dot(p.astype(BF16), v_ref[:, hs].astype(BF16)) / jnp.sum(p, axis=-1, keepdims=True)
        o_ref[:, hs] = (o * _silu(g_ref[:, hs])).astype(BF16)


def _ctx_attention(q, cache_k, cache_v, layer, z, batch, seq, w_a):
    t = q.shape[0]
    cache_spec = pl.BlockSpec((None, None, seq, w_a), lambda b: (b, layer, 0, 0))
    tok_spec = pl.BlockSpec((seq, w_a), lambda b: (b, 0))
    return pl.pallas_call(
        _ctx_attn_kernel,
        grid=(batch,),
        in_specs=[tok_spec, cache_spec, cache_spec, tok_spec],
        out_specs=tok_spec,
        out_shape=jax.ShapeDtypeStruct((t, w_a), BF16),
        compiler_params=_cparams(1, 40),
        name="ctx_attention",
    )(q, cache_k, cache_v, z)


def _swap_halves_32(x):
    lane = lax.broadcasted_iota(jnp.int32, x.shape, 1)
    return jnp.where((lane & 32) == 0, pltpu.roll(x, 96, 1), pltpu.roll(x, 32, 1))


def _qkv_kernel(a_ref, wq_ref, wk_ref, wv_ref, qg_ref, kg_ref, *rest, latent, chunk):
    q_out, k_out, v_out = rest[-3:]
    n_heads = wq_ref.shape[1] // HEAD_DIM_A
    for c in range(a_ref.shape[0] // chunk):
        rs = slice(c * chunk, (c + 1) * chunk)
        a = a_ref[rs, :]
        q = _dot(a, wq_ref[...])
        k = _dot(a, wk_ref[...])
        v = _dot(a, wv_ref[...])
        for h in range(n_heads):
            hs = slice(h * HEAD_DIM_A, (h + 1) * HEAD_DIM_A)
            qn = _rms(q[:, hs], qg_ref[...])
            kn = _rms(k[:, hs], kg_ref[...])
            if latent:
                cos = rest[0][rs, :]
                sin = rest[1][rs, :]
                q_out[rs, hs] = (qn * cos + _swap_halves_32(qn) * sin).astype(BF16)
                k_out[rs, hs] = (kn * cos + _swap_halves_32(kn) * sin).astype(BF16)
            else:
                q_out[rs, hs] = qn.astype(BF16)
                k_out[c, :, hs] = kn
        if latent:
            v_out[rs, :] = v.astype(BF16)
        else:
            v_out[c] = v


def _qkv_proj(h, w, layer, qg, kg, w_a, seq, cos=None, sin=None, batch=None, depth=None, caches=None):
    t, kdim = h.shape
    latent = cos is not None
    tm, tn = 1024, 512
    nb = w_a // tn
    wspec = lambda seg: pl.BlockSpec((None, kdim, tn), lambda i, j: (layer, 0, seg * nb + j))
    gspec = pl.BlockSpec((1, HEAD_DIM_A), lambda i, j: (0, 0))
    tok_spec = pl.BlockSpec((tm, tn), lambda i, j: (i, j))
    tok_shape = jax.ShapeDtypeStruct((t, w_a), BF16)
    args = [h, w, w, w, qg.reshape(1, -1), kg.reshape(1, -1)]
    in_specs = [pl.BlockSpec((tm, kdim), lambda i, j: (i, 0)), wspec(0), wspec(1), wspec(2), gspec, gspec]
    aliases = {}
    if latent:
        tab = pl.BlockSpec((tm, HEAD_DIM_A), lambda i, j: (i % (seq // tm), 0))
        args += [cos, sin]
        in_specs += [tab, tab]
        out_specs = [tok_spec] * 3
        out_shape = [tok_shape] * 3
    else:
        cache_spec = pl.BlockSpec((tm // seq, None, seq, tn), lambda i, j: (i, layer, 0, j))
        cache_shape = jax.ShapeDtypeStruct((batch, depth, seq, w_a), F32)
        if caches is not None:
            aliases = {len(args): 1, len(args) + 1: 2}
            args += list(caches)
            in_specs += [pl.BlockSpec(memory_space=pl.ANY)] * 2
        out_specs = [tok_spec, cache_spec, cache_spec]
        out_shape = [tok_shape, cache_shape, cache_shape]
    return pl.pallas_call(
        functools.partial(_qkv_kernel, latent=latent, chunk=256 if latent else seq),
        grid=(t // tm, nb),
        in_specs=in_specs,
        out_specs=out_specs,
        out_shape=out_shape,
        input_output_aliases=aliases,
        compiler_params=_cparams(2, 56),
        name="qkv_proj",
    )(*args)


def _rpb_table_kernel(rpb_ref, t_ref, ts_ref, blk_ref, *, kh, shifted_dr):
    n_dr = 2 * WIN_H - 1
    n_dc = 2 * WIN_W - 1
    h = pl.program_id(0)
    shape = (GRID_W, 2 * GRID_W)
    qcol = lax.broadcasted_iota(jnp.int32, shape, 0)
    lane = lax.broadcasted_iota(jnp.int32, shape, 1)
    kcol = lane & (GRID_W - 1)
    dc = jnp.clip(kcol - qcol, 1 - WIN_W, WIN_W - 1) + (WIN_W - 1)
    c0 = jnp.clip(qcol - WIN_W // 2, 0, GRID_W - WIN_W)
    in_win = (kcol >= c0) & (kcol < c0 + WIN_W)
    for dr in range(n_dr):
        acc = jnp.zeros(shape, F32)
        for j in range(n_dc):
            acc = jnp.where(dc == j, rpb_ref[(h * n_dr + dr) * n_dc + j], acc)
        blk_ref[dr] = jnp.where(in_win, acc, NEG_INF)

    left = lane < GRID_W
    neg = jnp.full(shape, NEG_INF, F32)

    def pair(a, b):
        return jnp.where(left, neg if a is None else blk_ref[a], neg if b is None else blk_ref[b])

    for d in range(WIN_H):
        for p in range(kh // 2):
            t_ref[d, 0, :, p * 2 * GRID_W:(p + 1) * 2 * GRID_W] = pair(d + 2 * p, d + 2 * p + 1)
    units = [None] + [shifted_dr + u for u in range(kh)] + [None]
    for p in range(len(units) // 2):
        ts_ref[0, :, p * 2 * GRID_W:(p + 1) * 2 * GRID_W] = pair(units[2 * p], units[2 * p + 1])


def _rpb_tables(rpb_l, kh, shifted_dr):
    n_heads = rpb_l.shape[0]
    assert 2 * GRID_W == 128 and kh % 2 == 0
    return pl.pallas_call(
        functools.partial(_rpb_table_kernel, kh=kh, shifted_dr=shifted_dr),
        grid=(n_heads,),
        in_specs=[pl.BlockSpec(memory_space=pltpu.SMEM)],
        out_specs=[pl.BlockSpec((WIN_H, 1, GRID_W, kh * GRID_W), lambda h: (0, h, 0, 0)),
                   pl.BlockSpec((1, GRID_W, (kh + 2) * GRID_W), lambda h: (h, 0, 0))],
        out_shape=[jax.ShapeDtypeStruct((WIN_H, n_heads, GRID_W, kh * GRID_W), F32),
                   jax.ShapeDtypeStruct((n_heads, GRID_W, (kh + 2) * GRID_W), F32)],
        scratch_shapes=[pltpu.VMEM((2 * WIN_H - 1, GRID_W, 2 * GRID_W), F32)],
        compiler_params=_cparams(1, 32),
        name="rpb_table",
    )(rpb_l.reshape(-1))


NA_QROWS = 4
NA_KROWS = NA_QROWS + WIN_H


def _na_block_plan(rows, kh):
    plans = {}
    for jb in range(rows // NA_QROWS):
        k0 = min(max(NA_QROWS * jb - kh // 2, 0), rows - NA_KROWS)
        plan = []
        for i in range(NA_QROWS):
            r = NA_QROWS * jb + i
            r0 = min(max(r - kh // 2, 0), rows - kh)
            assert 0 <= r0 - k0 <= NA_KROWS - kh
            plan.append((r0 - k0, r0 - r + WIN_H - 1))
        plans.setdefault(tuple(plan), []).append(jb)
    return plans


def _na_kernel(q_ref, k_ref, v_ref, kc_ref, vc_ref, g_ref, t_ref, ts_ref, o_ref, bias_ref, kcb_ref, vcb_ref,
               *, rows, kh, heads_per_step, shifted_dr):
    scale = HEAD_DIM_A ** -0.5
    jb = pl.program_id(1)
    span = kh * GRID_W

    @pl.when(pl.program_id(2) == 0)
    def _assemble():
        for plan, jbs in _na_block_plan(rows, kh).items():
            pred = functools.reduce(jnp.logical_or, [jb == j for j in jbs])

            @pl.when(pred)
            def _():
                for hh in range(heads_per_step):
                    for i, (off, dr0) in enumerate(plan):
                        rs = slice(i * GRID_W, (i + 1) * GRID_W)
                        bias_ref[hh, rs, :] = jnp.full((GRID_W, NA_KROWS * GRID_W), NEG_INF, F32)
                        if off % 2 == 0:
                            bias_ref[hh, rs, off * GRID_W:off * GRID_W + span] = t_ref[dr0, hh]
                        else:
                            assert dr0 == shifted_dr
                            lo = (off - 1) * GRID_W
                            bias_ref[hh, rs, lo:lo + span + 2 * GRID_W] = ts_ref[hh]

    kcb_ref[...] = kc_ref[...].astype(BF16)
    vcb_ref[...] = vc_ref[...].astype(BF16)
    k0 = jnp.clip(NA_QROWS * jb - kh // 2, 0, rows - NA_KROWS)
    ks = pl.ds(pl.multiple_of(k0 * GRID_W, NA_QROWS * GRID_W), NA_KROWS * GRID_W)
    for hh in range(heads_per_step):
        hs = slice(hh * HEAD_DIM_A, (hh + 1) * HEAD_DIM_A)
        q = q_ref[:, hs]
        s_loc = _dot_nt(q, k_ref[ks, hs]) * scale + bias_ref[hh]
        s_ctx = _dot_nt(q, kcb_ref[:, hs]) * scale
        m = jnp.maximum(jnp.max(s_loc, axis=-1, keepdims=True), jnp.max(s_ctx, axis=-1, keepdims=True))
        p_loc = jnp.exp(s_loc - m)
        p_ctx = jnp.exp(s_ctx - m)
        den = jnp.sum(p_loc, axis=-1, keepdims=True) + jnp.sum(p_ctx, axis=-1, keepdims=True)
        o = (_dot(p_loc.astype(BF16), v_ref[ks, hs]) + _dot(p_ctx.astype(BF16), vcb_ref[:, hs])) / den
        o_ref[:, hs] = (o * _silu(g_ref[:, hs])).astype(BF16)


def _nbr_attention(qr, kr, vb, cache_k, cache_v, layer, z, rpb_l, batch, seq, w_a):
    t = qr.shape[0]
    rows = seq // GRID_W
    kh = min(WIN_H, rows)
    assert rows % NA_QROWS == 0 and rows >= NA_KROWS and NA_QROWS % 2 == 0
    heads_per_step = 4
    wb = heads_per_step * HEAD_DIM_A
    tq = NA_QROWS * GRID_W
    steps_per_seq = seq // tq
    lc = cache_k.shape[2]
    ga_col0 = 0
    shifted_dr = WIN_H - 1 - kh // 2
    bias, bias_shifted = _rpb_tables(rpb_l, kh, shifted_dr)
    kern = functools.partial(_na_kernel, rows=rows, kh=kh, heads_per_step=heads_per_step, shifted_dr=shifted_dr)
    qspec = pl.BlockSpec((tq, wb), lambda g, r, b: (b * steps_per_seq + r, g))
    kvspec = pl.BlockSpec((seq, wb), lambda g, r, b: (b, g))
    cspec = pl.BlockSpec((None, None, lc, wb), lambda g, r, b: (b, layer, 0, g))
    return pl.pallas_call(
        kern,
        grid=(w_a // wb, steps_per_seq, batch),
        in_specs=[
            qspec, kvspec, kvspec, cspec, cspec,
            pl.BlockSpec((tq, wb), lambda g, r, b: (b * steps_per_seq + r, ga_col0 + g)),
            pl.BlockSpec((WIN_H, heads_per_step, GRID_W, kh * GRID_W), lambda g, r, b: (0, g, 0, 0)),
            pl.BlockSpec((heads_per_step, GRID_W, (kh + 2) * GRID_W), lambda g, r, b: (g, 0, 0)),
        ],
        out_specs=qspec,
        out_shape=jax.ShapeDtypeStruct((t, w_a), BF16),
        scratch_shapes=[pltpu.VMEM((heads_per_step, tq, NA_KROWS * GRID_W), F32),
                        pltpu.VMEM((lc, wb), BF16), pltpu.VMEM((lc, wb), BF16)],
        compiler_params=_cparams(3, 48),
        name="nbr_attention",
    )(qr, kr, vb, cache_k, cache_v, z, bias, bias_shifted)


def _conv_kernel(u_ref, gl_ref, up_ref, glp_ref, un_ref, gln_ref, gb_ref, cw_ref, cb_ref, lng_ref, lnb_ref,
                 wpw_ref, o_ref, hp_ref, hs_ref, acc_ref, *, tiles_per_seq):
    tl, wb = u_ref.shape
    i = pl.program_id(0)
    first = (i % tiles_per_seq) == 0
    last = (i % tiles_per_seq) == tiles_per_seq - 1
    halo = CONV_HALO
    hp_ref[0:halo, :] = jnp.where(first, 0.0, up_ref[...] * jax.nn.sigmoid(glp_ref[...]))
    hp_ref[halo:halo + tl, :] = u_ref[...] * jax.nn.sigmoid(gl_ref[...])
    hp_ref[halo + tl:2 * halo + tl, :] = jnp.where(last, 0.0, un_ref[...] * jax.nn.sigmoid(gln_ref[...]))
    span = hs_ref.shape[1]
    for s in range(8):
        hs_ref[s] = hp_ref[s:s + span, :]

    rt = 128
    lanes = 128
    off = halo - CONV_K // 2

    for cc in range(wb // lanes):
        cs = slice(cc * lanes, (cc + 1) * lanes)

        def row_body(ri, carry, cs=cs):
            r0 = pl.multiple_of(ri * rt, rt)
            acc = jnp.broadcast_to(cb_ref[:, cs], (rt, lanes))
            for s in range(8):
                taps = [k for k in range(CONV_K) if (k + off) % 8 == s]
                reach = 8 * max((k + off) // 8 for k in taps)
                win = hs_ref[s, pl.ds(r0, rt + reach), cs]
                for k in taps:
                    a = (k + off) // 8
                    acc = acc + win[8 * a:8 * a + rt] * cw_ref[k:k + 1, cs]
            acc_ref[pl.ds(r0, rt), cs] = acc
            return carry

        lax.fori_loop(0, tl // rt, row_body, 0)

    y = acc_ref[...]
    mu = jnp.mean(y, axis=-1, keepdims=True)
    yc = y - mu
    var = jnp.mean(yc * yc, axis=-1, keepdims=True)
    yn = yc * lax.rsqrt(var + EPS) * lng_ref[...] + lnb_ref[...]
    ob = _dot(_silu(yn).astype(BF16), wpw_ref[...])
    o_ref[...] = (ob * _silu(gb_ref[...])).astype(BF16)


def _conformer(z, conv_w, conv_b, ln_g, ln_b, w_pw, layer, seq, w_b, col0):
    t = z.shape[0]
    tl = min(seq, 256)
    tiles_per_seq = seq // tl
    hb = tl // CONV_HALO
    n_hb = t // CONV_HALO
    c_u, c_gl, c_gb = col0 // w_b, col0 // w_b + 1, col0 // w_b + 2
    main = lambda c: pl.BlockSpec((tl, w_b), lambda i: (i, c))
    prev = lambda c: pl.BlockSpec((CONV_HALO, w_b), lambda i: (jnp.maximum(i * hb - 1, 0), c))
    nxt = lambda c: pl.BlockSpec((CONV_HALO, w_b), lambda i: (jnp.minimum((i + 1) * hb, n_hb - 1), c))
    row = pl.BlockSpec((1, w_b), lambda i: (0, 0))
    span = tl + 2 * CONV_HALO - 8
    kern = functools.partial(_conv_kernel, tiles_per_seq=tiles_per_seq)
    return pl.pallas_call(
        kern,
        grid=(t // tl,),
        in_specs=[
            main(c_u), main(c_gl), prev(c_u), prev(c_gl), nxt(c_u), nxt(c_gl), main(c_gb),
            pl.BlockSpec((CONV_K, w_b), lambda i: (0, 0)), row, row, row,
            pl.BlockSpec((None, w_b, w_b), lambda i: (layer, 0, 0)),
        ],
        out_specs=pl.BlockSpec((tl, w_b), lambda i: (i, 0)),
        out_shape=jax.ShapeDtypeStruct((t, w_b), BF16),
        scratch_shapes=[
            pltpu.VMEM((tl + 2 * CONV_HALO, w_b), F32),
            pltpu.VMEM((8, span, w_b), F32),
            pltpu.VMEM((tl, w_b), F32),
        ],
        compiler_params=_cparams(1, 48),
        name="conformer",
    )(z, z, z, z, z, z, z, conv_w, conv_b.reshape(1, -1), ln_g.reshape(1, -1), ln_b.reshape(1, -1), w_pw)


def _swap_halves_128(x):
    half = x.shape[1] // 2
    return jnp.concatenate([x[:, half:], x[:, :half]], axis=1)


def _ret_kernel(coef_ref, q_ref, k_ref, v_ref, g_ref, gn_ref, *rest, n_chunks, n_heads, latent):
    n_scratch = 10
    scratch = rest[-n_scratch:]
    if latent:
        cos_ref, sin_ref, s0_ref, o_ref = rest[:4]
    else:
        o_ref, so_ref = rest[-n_scratch - 2:-n_scratch]
    m_ref, xf_ref, xb_ref, zf_ref, zb_ref, sf_ref, sb_ref, oacc_ref, qs_ref, ks_ref = scratch
    c = RET_CHUNK
    h = pl.program_id(0)
    lgf = coef_ref[h]
    lgb = coef_ref[n_heads + h]
    dsf = coef_ref[2 * n_heads + h]
    dsb = coef_ref[3 * n_heads + h]

    @pl.when(pl.program_id(1) == 0)
    def _tables():
        rows = lax.broadcasted_iota(jnp.int32, (c, c), 0).astype(F32)
        cols = lax.broadcasted_iota(jnp.int32, (c, c), 1).astype(F32)
        diff = rows - cols
        m_ref[...] = (jnp.where(diff >= 0, jnp.exp(lgf * jnp.maximum(diff, 0.0)), 0.0) * dsf
                      + jnp.where(diff <= 0, jnp.exp(lgb * jnp.maximum(-diff, 0.0)), 0.0) * dsb)
        xf_ref[...] = dsf * jnp.exp(lgf * (rows + 1.0))
        xb_ref[...] = dsb * jnp.exp(lgb * (c - rows))
        zf_ref[...] = jnp.exp(lgf * (c - 1.0 - rows))
        zb_ref[...] = jnp.exp(lgb * rows)

    chunk_len = jnp.full((1, c), float(c), F32)
    gcf = jnp.exp(lgf * chunk_len)
    gcb = jnp.exp(lgb * chunk_len)

    k_scale = HEAD_DIM_C ** -0.5
    if not latent and n_chunks == 1:
        for si in range(q_ref.shape[0] // c):
            rs = slice(si * c, (si + 1) * c)
            k = k_ref[rs, :] * k_scale
            vb = v_ref[rs, :].astype(BF16)
            a = _dot_nt(q_ref[rs, :].astype(BF16), k.astype(BF16))
            o = _dot((a * m_ref[...]).astype(BF16), vb)
            so_ref[si, 0] = _dot((k * zf_ref[...]).T.astype(BF16), vb)
            so_ref[si, 1] = _dot((k * zb_ref[...]).T.astype(BF16), vb)
            o_ref[rs, :] = (_rms(o, gn_ref[...]) * _silu(g_ref[rs, :])).astype(BF16)
        return

    if latent:
        sf_ref[...] = s0_ref[0]
        sb_ref[...] = s0_ref[1]
    else:
        sf_ref[...] = jnp.zeros((c, c), F32)
        sb_ref[...] = jnp.zeros((c, c), F32)
    use_cross = latent or n_chunks > 1

    def forward(ci, carry):
        sl = pl.ds(pl.multiple_of(ci * c, c), c)
        q = q_ref[sl, :]
        k = k_ref[sl, :] * k_scale
        if latent:
            cos = cos_ref[sl, :]
            sin = sin_ref[sl, :]
            q = q * cos + _swap_halves_128(q) * sin
            k = k * cos + _swap_halves_128(k) * sin
        qb = q.astype(BF16)
        vb = v_ref[sl, :].astype(BF16)
        qs_ref[sl, :] = qb
        ks_ref[sl, :] = k
        a = _dot_nt(qb, k.astype(BF16))
        o = _dot((a * m_ref[...]).astype(BF16), vb)
        if use_cross:
            o = o + _dot(qb, sf_ref[...].astype(BF16)) * xf_ref[...]
        oacc_ref[sl, :] = o
        sf_ref[...] = gcf * sf_ref[...] + _dot((k * zf_ref[...]).T.astype(BF16), vb)
        return carry

    def backward(cj, carry):
        ci = n_chunks - 1 - cj
        sl = pl.ds(pl.multiple_of(ci * c, c), c)
        k = ks_ref[sl, :]
        vb = v_ref[sl, :].astype(BF16)
        o = oacc_ref[sl, :]
        if use_cross:
            o = o + _dot(qs_ref[sl, :], sb_ref[...].astype(BF16)) * xb_ref[...]
        sb_ref[...] = gcb * sb_ref[...] + _dot((k * zb_ref[...]).T.astype(BF16), vb)
        o_ref[sl, :] = (_rms(o, gn_ref[...]) * _silu(g_ref[sl, :])).astype(BF16)
        return carry

    lax.fori_loop(0, n_chunks, forward, 0)
    lax.fori_loop(0, n_chunks, backward, 0)
    if not latent:
        so_ref[0, 0] = sf_ref[...]
        so_ref[0, 1] = sb_ref[...]


def _retention(z, coef, gn_g, batch, seq, n_heads, col0, layer, depth=None, cos=None, sin=None, state=None,
               prev_states=None):
    t = z.shape[0]
    dh = HEAD_DIM_C
    latent = state is not None
    n_chunks = seq // RET_CHUNK
    cb = col0 // dh
    seqs = 4 if (not latent and n_chunks == 1 and batch % 4 == 0) else 1
    rows = seqs * seq
    zspec = lambda seg: pl.BlockSpec((rows, dh), lambda h, b: (b, cb + seg * n_heads + h))
    in_specs = [
        pl.BlockSpec(memory_space=pltpu.SMEM),
        zspec(0), zspec(1), zspec(2), zspec(3),
        pl.BlockSpec((1, dh), lambda h, b: (0, h)),
    ]
    args = [coef, z, z, z, z, gn_g.reshape(1, -1)]
    ospec = pl.BlockSpec((rows, dh), lambda h, b: (b, h))
    oshape = jax.ShapeDtypeStruct((t, n_heads * dh), BF16)
    aliases = {}
    if latent:
        tab = pl.BlockSpec((seq, dh), lambda h, b: (0, 0))
        state_spec = pl.BlockSpec((None, None, 2, None, dh, dh), lambda h, b: (b, layer, 0, h, 0, 0))
        in_specs += [tab, tab, state_spec]
        args += [cos, sin, state]
        out_specs, out_shape = ospec, oshape
    else:
        if prev_states is not None:
            aliases = {len(args): 1}
            args.append(prev_states)
            in_specs.append(pl.BlockSpec(memory_space=pl.ANY))
        state_spec = pl.BlockSpec((seqs, None, 2, None, dh, dh), lambda h, b: (b, layer, 0, h, 0, 0))
        out_specs = [ospec, state_spec]
        out_shape = [oshape, jax.ShapeDtypeStruct((batch, depth, 2, n_heads, dh, dh), F32)]
    sq = lambda: pltpu.VMEM((RET_CHUNK, RET_CHUNK), F32)
    kern = functools.partial(_ret_kernel, n_chunks=n_chunks, n_heads=n_heads, latent=latent)
    return pl.pallas_call(
        kern,
        grid=(n_heads, batch // seqs),
        in_specs=in_specs,
        out_specs=out_specs,
        out_shape=out_shape,
        input_output_aliases=aliases,
        scratch_shapes=[sq(), sq(), sq(), sq(), sq(), sq(), sq(),
                        pltpu.VMEM((seq, dh), F32), pltpu.VMEM((seq, dh), BF16), pltpu.VMEM((seq, dh), F32)],
        compiler_params=_cparams(2, 48),
        name="retention",
    )(*args)


def _out_proj_kernel(oa_ref, ob_ref, oc_ref, w_ref, x_ref, m_ref, y_ref, *, row0, tiles_per_cond):
    wa = oa_ref.shape[1]
    wb = ob_ref.shape[1]
    row = row0 + pl.program_id(0) // tiles_per_cond
    acc = _dot(oa_ref[...], w_ref[0:wa, :])
    acc = acc + _dot(ob_ref[...], w_ref[wa:wa + wb, :])
    acc = acc + _dot(oc_ref[...], w_ref[wa + wb:, :])
    y_ref[...] = x_ref[...] + m_ref[pl.ds(row, 1), :] * acc


def _out_proj(oa, ob, oc, w, x, mod, layer, row0, tokens_per_cond):
    t, d = x.shape
    tm, tn = 1024, 1024
    gate_col0 = (2 * d) // tn
    kern = functools.partial(_out_proj_kernel, row0=row0, tiles_per_cond=tokens_per_cond // tm)
    act = lambda a: pl.BlockSpec((tm, a.shape[1]), lambda i, j: (i, 0))
    return pl.pallas_call(
        kern,
        grid=(t // tm, d // tn),
        in_specs=[
            act(oa), act(ob), act(oc),
            pl.BlockSpec((None, d, tn), lambda i, j: (layer, 0, j)),
            pl.BlockSpec((tm, tn), lambda i, j: (i, j)),
            pl.BlockSpec((None, COND_ROWS, tn), lambda i, j: (layer, 0, gate_col0 + j)),
        ],
        out_specs=pl.BlockSpec((tm, tn), lambda i, j: (i, j)),
        out_shape=jax.ShapeDtypeStruct((t, d), F32),
        compiler_params=_cparams(2, 56),
        name="out_proj",
    )(oa, ob, oc, w, x, mod)


def _rope_angles(pos, n_dims):
    half = n_dims // 2
    freqs = ROPE_BASE ** (-jnp.arange(half, dtype=F32) / half)
    return pos.astype(F32)[:, None] * freqs[None, :]


def _axial_tables(seq):
    t = jnp.arange(seq)
    ar = _rope_angles(t // GRID_W, HEAD_DIM_A // 2)
    ac = _rope_angles(t % GRID_W, HEAD_DIM_A // 2)
    cos = jnp.concatenate([jnp.cos(ar), jnp.cos(ar), jnp.cos(ac), jnp.cos(ac)], -1)
    sin = jnp.concatenate([-jnp.sin(ar), jnp.sin(ar), -jnp.sin(ac), jnp.sin(ac)], -1)
    return cos, sin


def _ret_tables(seq):
    ang = _rope_angles(jnp.arange(seq), HEAD_DIM_C)
    cos = jnp.concatenate([jnp.cos(ang), jnp.cos(ang)], -1)
    sin = jnp.concatenate([-jnp.sin(ang), jnp.sin(ang)], -1)
    return cos, sin


def kernel(x_prompt, x_sample, cache_k, cache_v, state_ret, c, c_ctx, norm_g, w_mod, b_mod, w_in, qn_g, kn_g,
           rpb, conv_w, conv_b, cln_g, cln_b, w_pw, ret_decay_logit, ret_dir_scale, ret_gn_g, w_out):
    batch, seq, d = x_prompt.shape
    dec_batch, dec_seq, _ = x_sample.shape
    depth = w_in.shape[0]
    n_heads_a = cache_k.shape[3]
    w_a = n_heads_a * HEAD_DIM_A
    w_b = w_pw.shape[1]
    n_heads_c = state_ret.shape[3]
    w_c = n_heads_c * HEAD_DIM_C
    col_b = w_a
    col_c = col_b + 3 * w_b
    assert w_in.shape[2] == 3 * w_a + col_c + 4 * w_c and w_a + w_b + w_c == d
    assert 1 + dec_batch <= COND_ROWS
    t_ctx = batch * seq
    t_lat = dec_batch * dec_seq

    cond = jnp.concatenate([c_ctx[None], c, jnp.zeros((COND_ROWS - 1 - dec_batch, d), F32)], axis=0)
    mod = _modulation(cond, w_mod, b_mod)
    w_in_b = w_in.astype(BF16)
    w_out_b = w_out.astype(BF16)
    w_pw_b = w_pw.astype(BF16)
    cache_k2 = cache_k.reshape(dec_batch, depth, -1, w_a)
    cache_v2 = cache_v.reshape(dec_batch, depth, -1, w_a)
    ax_cos, ax_sin = _axial_tables(dec_seq)
    rt_cos, rt_sin = _ret_tables(dec_seq)
    log_gamma = jax.nn.log_sigmoid(ret_decay_logit.astype(F32))
    ret_coef = jnp.concatenate([log_gamma, ret_dir_scale.astype(F32)], axis=1).reshape(depth, -1)

    xp = x_prompt.reshape(t_ctx, d)
    xs = x_sample.reshape(t_lat, d)
    caches = None
    states = None
    for l in range(depth):
        conv_args = (conv_w[l], conv_b[l], cln_g[l], cln_b[l], w_pw_b, l)
        h = _norm_mod(xp, norm_g[l], mod, l, 0, t_ctx)
        q, *caches = _qkv_proj(h, w_in_b, l, qn_g[l], kn_g[l], w_a, seq, batch=batch, depth=depth, caches=caches)
        z = _in_proj(h, w_in_b, l, 3 * w_a)
        oa = _ctx_attention(q, *caches, l, z, batch, seq, w_a)
        ob = _conformer(z, *conv_args, seq, w_b, col_b)
        oc, states = _retention(z, ret_coef[l], ret_gn_g[l], batch, seq, n_heads_c, col_c, l, depth=depth,
                                prev_states=states)
        xp = _out_proj(oa, ob, oc, w_out_b, xp, mod, l, 0, t_ctx)
        h = _norm_mod(xs, norm_g[l], mod, l, 1, dec_seq)
        qr, kr, vb = _qkv_proj(h, w_in_b, l, qn_g[l], kn_g[l], w_a, dec_seq, cos=ax_cos, sin=ax_sin)
        z = _in_proj(h, w_in_b, l, 3 * w_a)
        oa = _nbr_attention(qr, kr, vb, cache_k2, cache_v2, l, z, rpb[l], dec_batch, dec_seq, w_a)
        ob = _conformer(z, *conv_args, dec_seq, w_b, col_b)
        oc = _retention(z, ret_coef[l], ret_gn_g[l], dec_batch, dec_seq, n_heads_c, col_c, l,
                        cos=rt_cos, sin=rt_sin, state=state_ret)
        xs = _out_proj(oa, ob, oc, w_out_b, xs, mod, l, 1, dec_seq)
    new_k, new_v = (a.reshape(batch, depth, seq, n_heads_a, HEAD_DIM_A) for a in caches)
    return (xp.reshape(batch, seq, d), xs.reshape(dec_batch, dec_seq, d), new_k, new_v, states)
```

```python
import functools

import jax
import jax.numpy as jnp
from jax import lax
from jax.experimental import pallas as pl
from jax.experimental.pallas import tpu as pltpu

F32 = jnp.float32
BF16 = jnp.bfloat16

GRID_W = 64
WIN_H = 8
WIN_W = 16
CONV_K = 31
HEAD_DIM_A = 128
HEAD_DIM_C = 256
ROPE_BASE = 10000.0
EPS = 1e-6
NEG_INF = -1e30

RET_CHUNK = 256
CONV_HALO = 16
COND_ROWS = 8
MIB = 1024 * 1024


def _cparams(n_axes, vmem_mib):
    return pltpu.CompilerParams(
        dimension_semantics=("arbitrary",) * n_axes,
        vmem_limit_bytes=vmem_mib * MIB,
    )


def _silu(x):
    return x * jax.nn.sigmoid(x)


def _dot(a, b):
    return jnp.dot(a, b, preferred_element_type=F32)


def _dot_nt(a, b):
    return lax.dot_general(a, b, (((1,), (1,)), ((), ())), preferred_element_type=F32)


def _rms(x, g):
    return x * lax.rsqrt(jnp.mean(x * x, axis=-1, keepdims=True) + EPS) * g


def _mod_kernel(cond_ref, w_ref, b_ref, o_ref):
    s = _silu(cond_ref[...]).astype(BF16)
    o_ref[0] = _dot(s, w_ref[0].astype(BF16)) + b_ref[0]


def _modulation(cond, w_mod, b_mod):
    depth, d, n = w_mod.shape
    tn = 512
    return pl.pallas_call(
        _mod_kernel,
        grid=(depth, n // tn),
        in_specs=[
            pl.BlockSpec((COND_ROWS, d), lambda l, j: (0, 0)),
            pl.BlockSpec((1, d, tn), lambda l, j: (l, 0, j)),
            pl.BlockSpec((1, 1, tn), lambda l, j: (l, 0, j)),
        ],
        out_specs=pl.BlockSpec((1, COND_ROWS, tn), lambda l, j: (l, 0, j)),
        out_shape=jax.ShapeDtypeStruct((depth, COND_ROWS, n), F32),
        compiler_params=_cparams(2, 40),
        name="modulation",
    )(cond, w_mod, b_mod.reshape(depth, 1, n))


def _norm_mod_kernel(x_ref, g_ref, m_ref, o_ref, *, row0, tiles_per_cond):
    d = x_ref.shape[1]
    row = row0 + pl.program_id(0) // tiles_per_cond
    shift = m_ref[pl.ds(row, 1), 0:d]
    scale = m_ref[pl.ds(row, 1), d:2 * d]
    y = _rms(x_ref[...], g_ref[...])
    o_ref[...] = (y * (1.0 + scale) + shift).astype(BF16)


def _norm_mod(x, g, mod, layer, row0, tokens_per_cond):
    t, d = x.shape
    tm = 512
    kern = functools.partial(_norm_mod_kernel, row0=row0, tiles_per_cond=tokens_per_cond // tm)
    return pl.pallas_call(
        kern,
        grid=(t // tm,),
        in_specs=[
            pl.BlockSpec((tm, d), lambda i: (i, 0)),
            pl.BlockSpec((1, d), lambda i: (0, 0)),
            pl.BlockSpec((None, COND_ROWS, 3 * d), lambda i: (layer, 0, 0)),
        ],
        out_specs=pl.BlockSpec((tm, d), lambda i: (i, 0)),
        out_shape=jax.ShapeDtypeStruct((t, d), BF16),
        compiler_params=_cparams(1, 40),
        name="norm_mod",
    )(x, g.reshape(1, d), mod)


def _matmul_kernel(a_ref, b_ref, o_ref):
    o_ref[...] = _dot(a_ref[...], b_ref[...])


def _in_proj(h, w, layer, col0):
    t, k = h.shape
    n = w.shape[2] - col0
    tm, tn = 1024, 1024
    cb0 = col0 // tn
    return pl.pallas_call(
        _matmul_kernel,
        grid=(t // tm, n // tn),
        in_specs=[
            pl.BlockSpec((tm, k), lambda i, j: (i, 0)),
            pl.BlockSpec((None, k, tn), lambda i, j: (layer, 0, cb0 + j)),
        ],
        out_specs=pl.BlockSpec((tm, tn), lambda i, j: (i, j)),
        out_shape=jax.ShapeDtypeStruct((t, n), F32),
        compiler_params=_cparams(2, 52),
        name="in_proj",
    )(h, w)


def _ctx_attn_kernel(q_ref, k_ref, v_ref, g_ref, o_ref):
    n_heads = q_ref.shape[1] // HEAD_DIM_A
    scale = HEAD_DIM_A ** -0.5
    for h in range(n_heads):
        hs = slice(h * HEAD_DIM_A, (h + 1) * HEAD_DIM_A)
        s = _dot_nt(q_ref[:, hs], k_ref[:, hs].astype(BF16)) * scale
        p = jnp.exp(s - jnp.max(s, axis=-1, keepdims=True))
        o = _dot(p.astype(BF16), v_ref[:, hs].astype(BF16)) / jnp.sum(p, axis=-1, keepdims=True)
        o_ref[:, hs] = (o * _silu(g_ref[:, hs])).astype(BF16)


def _ctx_attention(q, cache_k, cache_v, layer, z, batch, seq, w_a):
    t = q.shape[0]
    cache_spec = pl.BlockSpec((None, None, seq, w_a), lambda b: (b, layer, 0, 0))
    tok_spec = pl.BlockSpec((seq, w_a), lambda b: (b, 0))
    return pl.pallas_call(
        _ctx_attn_kernel,
        grid=(batch,),
        in_specs=[tok_spec, cache_spec, cache_spec, tok_spec],
        out_specs=tok_spec,
        out_shape=jax.ShapeDtypeStruct((t, w_a), BF16),
        compiler_params=_cparams(1, 40),
        name="ctx_attention",
    )(q, cache_k, cache_v, z)


def _swap_halves_32(x):
    lane = lax.broadcasted_iota(jnp.int32, x.shape, 1)
    return jnp.where((lane & 32) == 0, pltpu.roll(x, 96, 1), pltpu.roll(x, 32, 1))


def _qkv_kernel(a_ref, wq_ref, wk_ref, wv_ref, qg_ref, kg_ref, *rest, latent, chunk, layer_slot):
    q_out, k_out, v_out = rest[-3:]
    n_heads = wq_ref.shape[1] // HEAD_DIM_A
    if not latent:
        for slot in range(k_out.shape[1]):
            if slot != layer_slot:
                k_out[:, slot] = jnp.zeros_like(k_out[:, slot])
                v_out[:, slot] = jnp.zeros_like(v_out[:, slot])
    for c in range(a_ref.shape[0] // chunk):
        rs = slice(c * chunk, (c + 1) * chunk)
        a = a_ref[rs, :]
        q = _dot(a, wq_ref[...])
        k = _dot(a, wk_ref[...])
        v = _dot(a, wv_ref[...])
        for h in range(n_heads):
            hs = slice(h * HEAD_DIM_A, (h + 1) * HEAD_DIM_A)
            qn = _rms(q[:, hs], qg_ref[...])
            kn = _rms(k[:, hs], kg_ref[...])
            if latent:
                cos = rest[0][rs, :]
                sin = rest[1][rs, :]
                q_out[rs, hs] = (qn * cos + _swap_halves_32(qn) * sin).astype(BF16)
                k_out[rs, hs] = (kn * cos + _swap_halves_32(kn) * sin).astype(BF16)
            else:
                q_out[rs, hs] = qn.astype(BF16)
                k_out[c, layer_slot, :, hs] = kn
        if latent:
            v_out[rs, :] = v.astype(BF16)
        else:
            v_out[c, layer_slot] = v


def _qkv_proj(h, w, layer, qg, kg, w_a, seq, cos=None, sin=None, batch=None, depth=None, caches=None):
    t, kdim = h.shape
    latent = cos is not None
    creates_caches = not latent and caches is None
    tm, tn = (1024, 256) if creates_caches else (1024, 512)
    nb = w_a // tn
    layer_slot = 0
    wspec = lambda seg: pl.BlockSpec((None, kdim, tn), lambda i, j: (layer, 0, seg * nb + j))
    gspec = pl.BlockSpec((1, HEAD_DIM_A), lambda i, j: (0, 0))
    tok_spec = pl.BlockSpec((tm, tn), lambda i, j: (i, j))
    tok_shape = jax.ShapeDtypeStruct((t, w_a), BF16)
    args = [h, w, w, w, qg.reshape(1, -1), kg.reshape(1, -1)]
    in_specs = [pl.BlockSpec((tm, kdim), lambda i, j: (i, 0)), wspec(0), wspec(1), wspec(2), gspec, gspec]
    aliases = {}
    if latent:
        tab = pl.BlockSpec((tm, HEAD_DIM_A), lambda i, j: (i % (seq // tm), 0))
        args += [cos, sin]
        in_specs += [tab, tab]
        out_specs = [tok_spec] * 3
        out_shape = [tok_shape] * 3
    else:
        cache_shape = jax.ShapeDtypeStruct((batch, depth, seq, w_a), F32)
        if creates_caches:
            layer_slot = layer
            cache_spec = pl.BlockSpec((tm // seq, depth, seq, tn), lambda i, j: (i, 0, 0, j))
        else:
            cache_spec = pl.BlockSpec((tm // seq, 1, seq, tn), lambda i, j: (i, layer, 0, j))
            aliases = {len(args): 1, len(args) + 1: 2}
            args += list(caches)
            in_specs += [pl.BlockSpec(memory_space=pl.ANY)] * 2
        out_specs = [tok_spec, cache_spec, cache_spec]
        out_shape = [tok_shape, cache_shape, cache_shape]
    return pl.pallas_call(
        functools.partial(_qkv_kernel, latent=latent, chunk=256 if latent else seq, layer_slot=layer_slot),
        grid=(t // tm, nb),
        in_specs=in_specs,
        out_specs=out_specs,
        out_shape=out_shape,
        input_output_aliases=aliases,
        compiler_params=_cparams(2, 56),
        name="qkv_proj",
    )(*args)


def _rpb_table_kernel(rpb_ref, t_ref, ts_ref, blk_ref, *, kh, shifted_dr):
    n_dr = 2 * WIN_H - 1
    n_dc = 2 * WIN_W - 1
    h = pl.program_id(0)
    shape = (GRID_W, 2 * GRID_W)
    qcol = lax.broadcasted_iota(jnp.int32, shape, 0)
    lane = lax.broadcasted_iota(jnp.int32, shape, 1)
    kcol = lane & (GRID_W - 1)
    dc = jnp.clip(kcol - qcol, 1 - WIN_W, WIN_W - 1) + (WIN_W - 1)
    c0 = jnp.clip(qcol - WIN_W // 2, 0, GRID_W - WIN_W)
    in_win = (kcol >= c0) & (kcol < c0 + WIN_W)
    for dr in range(n_dr):
        acc = jnp.zeros(shape, F32)
        for j in range(n_dc):
            acc = jnp.where(dc == j, rpb_ref[(h * n_dr + dr) * n_dc + j], acc)
        blk_ref[dr] = jnp.where(in_win, acc, NEG_INF)

    left = lane < GRID_W
    neg = jnp.full(shape, NEG_INF, F32)

    def pair(a, b):
        return jnp.where(left, neg if a is None else blk_ref[a], neg if b is None else blk_ref[b])

    for d in range(WIN_H):
        for p in range(kh // 2):
            t_ref[d, 0, :, p * 2 * GRID_W:(p + 1) * 2 * GRID_W] = pair(d + 2 * p, d + 2 * p + 1)
    units = [None] + [shifted_dr + u for u in range(kh)] + [None]
    for p in range(len(units) // 2):
        ts_ref[0, :, p * 2 * GRID_W:(p + 1) * 2 * GRID_W] = pair(units[2 * p], units[2 * p + 1])


def _rpb_tables(rpb_l, kh, shifted_dr):
    n_heads = rpb_l.shape[0]
    assert 2 * GRID_W == 128 and kh % 2 == 0
    return pl.pallas_call(
        functools.partial(_rpb_table_kernel, kh=kh, shifted_dr=shifted_dr),
        grid=(n_heads,),
        in_specs=[pl.BlockSpec(memory_space=pltpu.SMEM)],
        out_specs=[pl.BlockSpec((WIN_H, 1, GRID_W, kh * GRID_W), lambda h: (0, h, 0, 0)),
                   pl.BlockSpec((1, GRID_W, (kh + 2) * GRID_W), lambda h: (h, 0, 0))],
        out_shape=[jax.ShapeDtypeStruct((WIN_H, n_heads, GRID_W, kh * GRID_W), F32),
                   jax.ShapeDtypeStruct((n_heads, GRID_W, (kh + 2) * GRID_W), F32)],
        scratch_shapes=[pltpu.VMEM((2 * WIN_H - 1, GRID_W, 2 * GRID_W), F32)],
        compiler_params=_cparams(1, 32),
        name="rpb_table",
    )(rpb_l.reshape(-1))


NA_QROWS = 4
NA_KROWS = NA_QROWS + WIN_H


def _na_block_plan(rows, kh):
    plans = {}
    for jb in range(rows // NA_QROWS):
        k0 = min(max(NA_QROWS * jb - kh // 2, 0), rows - NA_KROWS)
        plan = []
        for i in range(NA_QROWS):
            r = NA_QROWS * jb + i
            r0 = min(max(r - kh // 2, 0), rows - kh)
            assert 0 <= r0 - k0 <= NA_KROWS - kh
            plan.append((r0 - k0, r0 - r + WIN_H - 1))
        plans.setdefault(tuple(plan), []).append(jb)
    return plans


def _na_kernel(q_ref, k_ref, v_ref, kc_ref, vc_ref, g_ref, t_ref, ts_ref, o_ref, bias_ref, kcb_ref, vcb_ref,
               *, rows, kh, heads_per_step, shifted_dr):
    scale = HEAD_DIM_A ** -0.5
    jb = pl.program_id(2)
    span = kh * GRID_W

    for plan, jbs in _na_block_plan(rows, kh).items():
        pred = functools.reduce(jnp.logical_or, [jb == j for j in jbs])

        @pl.when(pred)
        def _():
            for hh in range(heads_per_step):
                for i, (off, dr0) in enumerate(plan):
                    rs = slice(i * GRID_W, (i + 1) * GRID_W)
                    bias_ref[hh, rs, :] = jnp.full((GRID_W, NA_KROWS * GRID_W), NEG_INF, F32)
                    if off % 2 == 0:
                        bias_ref[hh, rs, off * GRID_W:off * GRID_W + span] = t_ref[dr0, hh]
                    else:
                        assert dr0 == shifted_dr
                        lo = (off - 1) * GRID_W
                        bias_ref[hh, rs, lo:lo + span + 2 * GRID_W] = ts_ref[hh]

    kcb_ref[...] = kc_ref[...].astype(BF16)
    vcb_ref[...] = vc_ref[...].astype(BF16)
    k0 = jnp.clip(NA_QROWS * jb - kh // 2, 0, rows - NA_KROWS)
    ks = pl.ds(pl.multiple_of(k0 * GRID_W, NA_QROWS * GRID_W), NA_KROWS * GRID_W)
    for hh in range(heads_per_step):
        hs = slice(hh * HEAD_DIM_A, (hh + 1) * HEAD_DIM_A)
        q = q_ref[:, hs]
        s_loc = _dot_nt(q, k_ref[ks, hs]) * scale + bias_ref[hh]
        s_ctx = _dot_nt(q, kcb_ref[:, hs]) * scale
        m = jnp.maximum(jnp.max(s_loc, axis=-1, keepdims=True), jnp.max(s_ctx, axis=-1, keepdims=True))
        p_loc = jnp.exp(s_loc - m)
        p_ctx = jnp.exp(s_ctx - m)
        den = jnp.sum(p_loc, axis=-1, keepdims=True) + jnp.sum(p_ctx, axis=-1, keepdims=True)
        o = (_dot(p_loc.astype(BF16), v_ref[ks, hs]) + _dot(p_ctx.astype(BF16), vcb_ref[:, hs])) / den
        o_ref[:, hs] = (o * _silu(g_ref[:, hs])).astype(BF16)


def _nbr_attention(qr, kr, vb, cache_k, cache_v, layer, z, rpb_l, batch, seq, w_a):
    t = qr.shape[0]
    rows = seq // GRID_W
    kh = min(WIN_H, rows)
    assert rows % NA_QROWS == 0 and rows >= NA_KROWS and NA_QROWS % 2 == 0
    heads_per_step = 4
    wb = heads_per_step * HEAD_DIM_A
    tq = NA_QROWS * GRID_W
    steps_per_seq = seq // tq
    lc = cache_k.shape[2]
    ga_col0 = 0
    shifted_dr = WIN_H - 1 - kh // 2
    bias, bias_shifted = _rpb_tables(rpb_l, kh, shifted_dr)
    kern = functools.partial(_na_kernel, rows=rows, kh=kh, heads_per_step=heads_per_step, shifted_dr=shifted_dr)
    qspec = pl.BlockSpec((tq, wb), lambda g, b, r: (b * steps_per_seq + r, g))
    kvspec = pl.BlockSpec((seq, wb), lambda g, b, r: (b, g))
    cspec = pl.BlockSpec((None, None, lc, wb), lambda g, b, r: (b, layer, 0, g))
    return pl.pallas_call(
        kern,
        grid=(w_a // wb, batch, steps_per_seq),
        in_specs=[
            qspec, kvspec, kvspec, cspec, cspec,
            pl.BlockSpec((tq, wb), lambda g, b, r: (b * steps_per_seq + r, ga_col0 + g)),
            pl.BlockSpec((WIN_H, heads_per_step, GRID_W, kh * GRID_W), lambda g, b, r: (0, g, 0, 0)),
            pl.BlockSpec((heads_per_step, GRID_W, (kh + 2) * GRID_W), lambda g, b, r: (g, 0, 0)),
        ],
        out_specs=qspec,
        out_shape=jax.ShapeDtypeStruct((t, w_a), BF16),
        scratch_shapes=[pltpu.VMEM((heads_per_step, tq, NA_KROWS * GRID_W), F32),
                        pltpu.VMEM((lc, wb), BF16), pltpu.VMEM((lc, wb), BF16)],
        compiler_params=_cparams(3, 48),
        name="nbr_attention",
    )(qr, kr, vb, cache_k, cache_v, z, bias, bias_shifted)


def _conv_kernel(u_ref, gl_ref, up_ref, glp_ref, un_ref, gln_ref, gb_ref, cw_ref, cb_ref, lng_ref, lnb_ref,
                 wpw_ref, o_ref, hp_ref, hs_ref, acc_ref, *, tiles_per_seq):
    tl, wb = u_ref.shape
    i = pl.program_id(0)
    first = (i % tiles_per_seq) == 0
    last = (i % tiles_per_seq) == tiles_per_seq - 1
    halo = CONV_HALO
    hp_ref[0:halo, :] = jnp.where(first, 0.0, up_ref[...] * jax.nn.sigmoid(glp_ref[...]))
    hp_ref[halo:halo + tl, :] = u_ref[...] * jax.nn.sigmoid(gl_ref[...])
    hp_ref[halo + tl:2 * halo + tl, :] = jnp.where(last, 0.0, un_ref[...] * jax.nn.sigmoid(gln_ref[...]))
    span = hs_ref.shape[1]
    for s in range(8):
        hs_ref[s] = hp_ref[s:s + span, :]

    rt = 128
    lanes = 128
    off = halo - CONV_K // 2

    for cc in range(wb // lanes):
        cs = slice(cc * lanes, (cc + 1) * lanes)

        def row_body(ri, carry, cs=cs):
            r0 = pl.multiple_of(ri * rt, rt)
            acc = jnp.broadcast_to(cb_ref[:, cs], (rt, lanes))
            for s in range(8):
                taps = [k for k in range(CONV_K) if (k + off) % 8 == s]
                reach = 8 * max((k + off) // 8 for k in taps)
                win = hs_ref[s, pl.ds(r0, rt + reach), cs]
                for k in taps:
                    a = (k + off) // 8
                    acc = acc + win[8 * a:8 * a + rt] * cw_ref[k:k + 1, cs]
            acc_ref[pl.ds(r0, rt), cs] = acc
            return carry

        lax.fori_loop(0, tl // rt, row_body, 0)

    y = acc_ref[...]
    mu = jnp.mean(y, axis=-1, keepdims=True)
    yc = y - mu
    var = jnp.mean(yc * yc, axis=-1, keepdims=True)
    yn = yc * lax.rsqrt(var + EPS) * lng_ref[...] + lnb_ref[...]
    ob = _dot(_silu(yn).astype(BF16), wpw_ref[...])
    o_ref[...] = (ob * _silu(gb_ref[...])).astype(BF16)


def _conformer(z, conv_w, conv_b, ln_g, ln_b, w_pw, layer, seq, w_b, col0):
    t = z.shape[0]
    tl = min(seq, 256)
    tiles_per_seq = seq // tl
    hb = tl // CONV_HALO
    n_hb = t // CONV_HALO
    c_u, c_gl, c_gb = col0 // w_b, col0 // w_b + 1, col0 // w_b + 2
    main = lambda c: pl.BlockSpec((tl, w_b), lambda i: (i, c))
    prev = lambda c: pl.BlockSpec((CONV_HALO, w_b), lambda i: (jnp.maximum(i * hb - 1, 0), c))
    nxt = lambda c: pl.BlockSpec((CONV_HALO, w_b), lambda i: (jnp.minimum((i + 1) * hb, n_hb - 1), c))
    row = pl.BlockSpec((1, w_b), lambda i: (0, 0))
    span = tl + 2 * CONV_HALO - 8
    kern = functools.partial(_conv_kernel, tiles_per_seq=tiles_per_seq)
    return pl.pallas_call(
        kern,
        grid=(t // tl,),
        in_specs=[
            main(c_u), main(c_gl), prev(c_u), prev(c_gl), nxt(c_u), nxt(c_gl), main(c_gb),
            pl.BlockSpec((CONV_K, w_b), lambda i: (0, 0)), row, row, row,
            pl.BlockSpec((None, w_b, w_b), lambda i: (layer, 0, 0)),
        ],
        out_specs=pl.BlockSpec((tl, w_b), lambda i: (i, 0)),
        out_shape=jax.ShapeDtypeStruct((t, w_b), BF16),
        scratch_shapes=[
            pltpu.VMEM((tl + 2 * CONV_HALO, w_b), F32),
            pltpu.VMEM((8, span, w_b), F32),
            pltpu.VMEM((tl, w_b), F32),
        ],
        compiler_params=_cparams(1, 48),
        name="conformer",
    )(z, z, z, z, z, z, z, conv_w, conv_b.reshape(1, -1), ln_g.reshape(1, -1), ln_b.reshape(1, -1), w_pw)


def _swap_halves_128(x):
    half = x.shape[1] // 2
    return jnp.concatenate([x[:, half:], x[:, :half]], axis=1)


def _ret_kernel(coef_ref, q_ref, k_ref, v_ref, g_ref, gn_ref, *rest, n_chunks, n_heads, latent, layer_slot):
    n_scratch = 10
    scratch = rest[-n_scratch:]
    if latent:
        cos_ref, sin_ref, s0_ref, o_ref = rest[:4]
    else:
        o_ref, so_ref = rest[-n_scratch - 2:-n_scratch]
        for slot in range(so_ref.shape[1]):
            if slot != layer_slot:
                so_ref[:, slot] = jnp.zeros_like(so_ref[:, slot])
    m_ref, xf_ref, xb_ref, zf_ref, zb_ref, sf_ref, sb_ref, oacc_ref, qs_ref, ks_ref = scratch
    c = RET_CHUNK
    h = pl.program_id(0)
    lgf = coef_ref[h]
    lgb = coef_ref[n_heads + h]
    dsf = coef_ref[2 * n_heads + h]
    dsb = coef_ref[3 * n_heads + h]

    @pl.when(pl.program_id(1) == 0)
    def _tables():
        rows = lax.broadcasted_iota(jnp.int32, (c, c), 0).astype(F32)
        cols = lax.broadcasted_iota(jnp.int32, (c, c), 1).astype(F32)
        diff = rows - cols
        m_ref[...] = (jnp.where(diff >= 0, jnp.exp(lgf * jnp.maximum(diff, 0.0)), 0.0) * dsf
                      + jnp.where(diff <= 0, jnp.exp(lgb * jnp.maximum(-diff, 0.0)), 0.0) * dsb)
        xf_ref[...] = dsf * jnp.exp(lgf * (rows + 1.0))
        xb_ref[...] = dsb * jnp.exp(lgb * (c - rows))
        zf_ref[...] = jnp.exp(lgf * (c - 1.0 - rows))
        zb_ref[...] = jnp.exp(lgb * rows)

    chunk_len = jnp.full((1, c), float(c), F32)
    gcf = jnp.exp(lgf * chunk_len)
    gcb = jnp.exp(lgb * chunk_len)

    k_scale = HEAD_DIM_C ** -0.5
    if not latent and n_chunks == 1:
        for si in range(q_ref.shape[0] // c):
            rs = slice(si * c, (si + 1) * c)
            k = k_ref[rs, :] * k_scale
            vb = v_ref[rs, :].astype(BF16)
            a = _dot_nt(q_ref[rs, :].astype(BF16), k.astype(BF16))
            o = _dot((a * m_ref[...]).astype(BF16), vb)
            so_ref[si, layer_slot, 0] = _dot((k * zf_ref[...]).T.astype(BF16), vb)
            so_ref[si, layer_slot, 1] = _dot((k * zb_ref[...]).T.astype(BF16), vb)
            o_ref[rs, :] = (_rms(o, gn_ref[...]) * _silu(g_ref[rs, :])).astype(BF16)
        return

    if latent:
        sf_ref[...] = s0_ref[0]
        sb_ref[...] = s0_ref[1]
    else:
        sf_ref[...] = jnp.zeros((c, c), F32)
        sb_ref[...] = jnp.zeros((c, c), F32)
    use_cross = latent or n_chunks > 1

    def forward(ci, carry):
        sl = pl.ds(pl.multiple_of(ci * c, c), c)
        q = q_ref[sl, :]
        k = k_ref[sl, :] * k_scale
        if latent:
            cos = cos_ref[sl, :]
            sin = sin_ref[sl, :]
            q = q * cos + _swap_halves_128(q) * sin
            k = k * cos + _swap_halves_128(k) * sin
        qb = q.astype(BF16)
        vb = v_ref[sl, :].astype(BF16)
        qs_ref[sl, :] = qb
        ks_ref[sl, :] = k
        a = _dot_nt(qb, k.astype(BF16))
        o = _dot((a * m_ref[...]).astype(BF16), vb)
        if use_cross:
            o = o + _dot(qb, sf_ref[...].astype(BF16)) * xf_ref[...]
        oacc_ref[sl, :] = o
        sf_ref[...] = gcf * sf_ref[...] + _dot((k * zf_ref[...]).T.astype(BF16), vb)
        return carry

    def backward(cj, carry):
        ci = n_chunks - 1 - cj
        sl = pl.ds(pl.multiple_of(ci * c, c), c)
        k = ks_ref[sl, :]
        vb = v_ref[sl, :].astype(BF16)
        o = oacc_ref[sl, :]
        if use_cross:
            o = o + _dot(qs_ref[sl, :], sb_ref[...].astype(BF16)) * xb_ref[...]
        sb_ref[...] = gcb * sb_ref[...] + _dot((k * zb_ref[...]).T.astype(BF16), vb)
        o_ref[sl, :] = (_rms(o, gn_ref[...]) * _silu(g_ref[sl, :])).astype(BF16)
        return carry

    lax.fori_loop(0, n_chunks, forward, 0)
    lax.fori_loop(0, n_chunks, backward, 0)
    if not latent:
        so_ref[0, layer_slot, 0] = sf_ref[...]
        so_ref[0, layer_slot, 1] = sb_ref[...]


def _retention(z, coef, gn_g, batch, seq, n_heads, col0, layer, depth=None, cos=None, sin=None, state=None,
               prev_states=None):
    t = z.shape[0]
    dh = HEAD_DIM_C
    latent = state is not None
    n_chunks = seq // RET_CHUNK
    cb = col0 // dh
    seqs = 4 if (not latent and n_chunks == 1 and batch % 4 == 0) else 1
    rows = seqs * seq
    zspec = lambda seg: pl.BlockSpec((rows, dh), lambda h, b: (b, cb + seg * n_heads + h))
    in_specs = [
        pl.BlockSpec(memory_space=pltpu.SMEM),
        zspec(0), zspec(1), zspec(2), zspec(3),
        pl.BlockSpec((1, dh), lambda h, b: (0, h)),
    ]
    args = [coef, z, z, z, z, gn_g.reshape(1, -1)]
    ospec = pl.BlockSpec((rows, dh), lambda h, b: (b, h))
    oshape = jax.ShapeDtypeStruct((t, n_heads * dh), BF16)
    aliases = {}
    layer_slot = 0
    if latent:
        tab = pl.BlockSpec((seq, dh), lambda h, b: (0, 0))
        state_spec = pl.BlockSpec((None, None, 2, None, dh, dh), lambda h, b: (b, layer, 0, h, 0, 0))
        in_specs += [tab, tab, state_spec]
        args += [cos, sin, state]
        out_specs, out_shape = ospec, oshape
    else:
        if prev_states is None:
            layer_slot = layer
            state_spec = pl.BlockSpec((seqs, depth, 2, None, dh, dh), lambda h, b: (b, 0, 0, h, 0, 0))
        else:
            state_spec = pl.BlockSpec((seqs, 1, 2, None, dh, dh), lambda h, b: (b, layer, 0, h, 0, 0))
            aliases = {len(args): 1}
            args.append(prev_states)
            in_specs.append(pl.BlockSpec(memory_space=pl.ANY))
        out_specs = [ospec, state_spec]
        out_shape = [oshape, jax.ShapeDtypeStruct((batch, depth, 2, n_heads, dh, dh), F32)]
    sq = lambda: pltpu.VMEM((RET_CHUNK, RET_CHUNK), F32)
    kern = functools.partial(_ret_kernel, n_chunks=n_chunks, n_heads=n_heads, latent=latent, layer_slot=layer_slot)
    return pl.pallas_call(
        kern,
        grid=(n_heads, batch // seqs),
        in_specs=in_specs,
        out_specs=out_specs,
        out_shape=out_shape,
        input_output_aliases=aliases,
        scratch_shapes=[sq(), sq(), sq(), sq(), sq(), sq(), sq(),
                        pltpu.VMEM((seq, dh), F32), pltpu.VMEM((seq, dh), BF16), pltpu.VMEM((seq, dh), F32)],
        compiler_params=_cparams(2, 48),
        name="retention",
    )(*args)


def _out_proj_kernel(oa_ref, ob_ref, oc_ref, w_ref, x_ref, m_ref, y_ref, *, row0, tiles_per_cond):
    wa = oa_ref.shape[1]
    wb = ob_ref.shape[1]
    row = row0 + pl.program_id(0) // tiles_per_cond
    acc = _dot(oa_ref[...], w_ref[0:wa, :])
    acc = acc + _dot(ob_ref[...], w_ref[wa:wa + wb, :])
    acc = acc + _dot(oc_ref[...], w_ref[wa + wb:, :])
    y_ref[...] = x_ref[...] + m_ref[pl.ds(row, 1), :] * acc


def _out_proj(oa, ob, oc, w, x, mod, layer, row0, tokens_per_cond):
    t, d = x.shape
    tm, tn = 1024, 1024
    gate_col0 = (2 * d) // tn
    kern = functools.partial(_out_proj_kernel, row0=row0, tiles_per_cond=tokens_per_cond // tm)
    act = lambda a: pl.BlockSpec((tm, a.shape[1]), lambda i, j: (i, 0))
    return pl.pallas_call(
        kern,
        grid=(t // tm, d // tn),
        in_specs=[
            act(oa), act(ob), act(oc),
            pl.BlockSpec((None, d, tn), lambda i, j: (layer, 0, j)),
            pl.BlockSpec((tm, tn), lambda i, j: (i, j)),
            pl.BlockSpec((None, COND_ROWS, tn), lambda i, j: (layer, 0, gate_col0 + j)),
        ],
        out_specs=pl.BlockSpec((tm, tn), lambda i, j: (i, j)),
        out_shape=jax.ShapeDtypeStruct((t, d), F32),
        compiler_params=_cparams(2, 56),
        name="out_proj",
    )(oa, ob, oc, w, x, mod)


def _rope_angles(pos, n_dims):
    half = n_dims // 2
    freqs = ROPE_BASE ** (-jnp.arange(half, dtype=F32) / half)
    return pos.astype(F32)[:, None] * freqs[None, :]


def _axial_tables(seq):
    t = jnp.arange(seq)
    ar = _rope_angles(t // GRID_W, HEAD_DIM_A // 2)
    ac = _rope_angles(t % GRID_W, HEAD_DIM_A // 2)
    cos = jnp.concatenate([jnp.cos(ar), jnp.cos(ar), jnp.cos(ac), jnp.cos(ac)], -1)
    sin = jnp.concatenate([-jnp.sin(ar), jnp.sin(ar), -jnp.sin(ac), jnp.sin(ac)], -1)
    return cos, sin


def _ret_tables(seq):
    ang = _rope_angles(jnp.arange(seq), HEAD_DIM_C)
    cos = jnp.concatenate([jnp.cos(ang), jnp.cos(ang)], -1)
    sin = jnp.concatenate([-jnp.sin(ang), jnp.sin(ang)], -1)
    return cos, sin


def kernel(x_prompt, x_sample, cache_k, cache_v, state_ret, c, c_ctx, norm_g, w_mod, b_mod, w_in, qn_g, kn_g,
           rpb, conv_w, conv_b, cln_g, cln_b, w_pw, ret_decay_logit, ret_dir_scale, ret_gn_g, w_out):
    batch, seq, d = x_prompt.shape
    dec_batch, dec_seq, _ = x_sample.shape
    depth = w_in.shape[0]
    n_heads_a = cache_k.shape[3]
    w_a = n_heads_a * HEAD_DIM_A
    w_b = w_pw.shape[1]
    n_heads_c = state_ret.shape[3]
    w_c = n_heads_c * HEAD_DIM_C
    col_b = w_a
    col_c = col_b + 3 * w_b
    assert w_in.shape[2] == 3 * w_a + col_c + 4 * w_c and w_a + w_b + w_c == d
    assert 1 + dec_batch <= COND_ROWS
    t_ctx = batch * seq
    t_lat = dec_batch * dec_seq

    cond = jnp.concatenate([c_ctx[None], c, jnp.zeros((COND_ROWS - 1 - dec_batch, d), F32)], axis=0)
    mod = _modulation(cond, w_mod, b_mod)
    w_in_b = w_in.astype(BF16)
    w_out_b = w_out.astype(BF16)
    w_pw_b = w_pw.astype(BF16)
    cache_k2 = cache_k.reshape(dec_batch, depth, -1, w_a)
    cache_v2 = cache_v.reshape(dec_batch, depth, -1, w_a)
    ax_cos, ax_sin = _axial_tables(dec_seq)
    rt_cos, rt_sin = _ret_tables(dec_seq)
    log_gamma = jax.nn.log_sigmoid(ret_decay_logit.astype(F32))
    ret_coef = jnp.concatenate([log_gamma, ret_dir_scale.astype(F32)], axis=1).reshape(depth, -1)

    xp = x_prompt.reshape(t_ctx, d)
    xs = x_sample.reshape(t_lat, d)
    caches = None
    states = None
    for l in range(depth):
        conv_args = (conv_w[l], conv_b[l], cln_g[l], cln_b[l], w_pw_b, l)
        h = _norm_mod(xp, norm_g[l], mod, l, 0, t_ctx)
        q, *caches = _qkv_proj(h, w_in_b, l, qn_g[l], kn_g[l], w_a, seq, batch=batch, depth=depth, caches=caches)
        z = _in_proj(h, w_in_b, l, 3 * w_a)
        oa = _ctx_attention(q, *caches, l, z, batch, seq, w_a)
        ob = _conformer(z, *conv_args, seq, w_b, col_b)
        oc, states = _retention(z, ret_coef[l], ret_gn_g[l], batch, seq, n_heads_c, col_c, l, depth=depth,
                                prev_states=states)
        xp = _out_proj(oa, ob, oc, w_out_b, xp, mod, l, 0, t_ctx)
        h = _norm_mod(xs, norm_g[l], mod, l, 1, dec_seq)
        qr, kr, vb = _qkv_proj(h, w_in_b, l, qn_g[l], kn_g[l], w_a, dec_seq, cos=ax_cos, sin=ax_sin)
        z = _in_proj(h, w_in_b, l, 3 * w_a)
        oa = _nbr_attention(qr, kr, vb, cache_k2, cache_v2, l, z, rpb[l], dec_batch, dec_seq, w_a)
        ob = _conformer(z, *conv_args, dec_seq, w_b, col_b)
        oc = _retention(z, ret_coef[l], ret_gn_g[l], dec_batch, dec_seq, n_heads_c, col_c, l,
                        cos=rt_cos, sin=rt_sin, state=state_ret)
        xs = _out_proj(oa, ob, oc, w_out_b, xs, mod, l, 1, dec_seq)
    new_k, new_v = (a.reshape(batch, depth, seq, n_heads_a, HEAD_DIM_A) for a in caches)
    return (xp.reshape(batch, seq, d), xs.reshape(dec_batch, dec_seq, d), new_k, new_v, states)
```

```python
import functools

import jax
import jax.numpy as jnp
from jax import lax
from jax.experimental import pallas as pl
from jax.experimental.pallas import tpu as pltpu

F32 = jnp.float32
BF16 = jnp.bfloat16

GRID_W = 64
WIN_H = 8
WIN_W = 16
CONV_K = 31
HEAD_DIM_A = 128
HEAD_DIM_C = 256
ROPE_BASE = 10000.0
EPS = 1e-6
NEG_INF = -1e30

RET_CHUNK = 256
CONV_HALO = 16
COND_ROWS = 8
MIB = 1024 * 1024


def _cparams(n_axes, vmem_mib):
    return pltpu.CompilerParams(
        dimension_semantics=("arbitrary",) * n_axes,
        vmem_limit_bytes=vmem_mib * MIB,
    )


def _silu(x):
    return x * jax.nn.sigmoid(x)


def _dot(a, b):
    return jnp.dot(a, b, preferred_element_type=F32)


def _dot_nt(a, b):
    return lax.dot_general(a, b, (((1,), (1,)), ((), ())), preferred_element_type=F32)


def _rms(x, g):
    return x * lax.rsqrt(jnp.mean(x * x, axis=-1, keepdims=True) + EPS) * g


def _mod_kernel(cond_ref, w_ref, b_ref, o_ref):
    s = _silu(cond_ref[...]).astype(BF16)
    o_ref[0] = _dot(s, w_ref[0].astype(BF16)) + b_ref[0]


def _modulation(cond, w_mod, b_mod):
    depth, d, n = w_mod.shape
    tn = 512
    return pl.pallas_call(
        _mod_kernel,
        grid=(depth, n // tn),
        in_specs=[
            pl.BlockSpec((COND_ROWS, d), lambda l, j: (0, 0)),
            pl.BlockSpec((1, d, tn), lambda l, j: (l, 0, j)),
            pl.BlockSpec((1, 1, tn), lambda l, j: (l, 0, j)),
        ],
        out_specs=pl.BlockSpec((1, COND_ROWS, tn), lambda l, j: (l, 0, j)),
        out_shape=jax.ShapeDtypeStruct((depth, COND_ROWS, n), F32),
        compiler_params=_cparams(2, 40),
        name="modulation",
    )(cond, w_mod, b_mod.reshape(depth, 1, n))


def _norm_mod_kernel(x_ref, g_ref, m_ref, o_ref, *, row0, tiles_per_cond):
    d = x_ref.shape[1]
    row = row0 + pl.program_id(0) // tiles_per_cond
    shift = m_ref[pl.ds(row, 1), 0:d]
    scale = m_ref[pl.ds(row, 1), d:2 * d]
    y = _rms(x_ref[...], g_ref[...])
    o_ref[...] = (y * (1.0 + scale) + shift).astype(BF16)


def _norm_mod(x, g, mod, layer, row0, tokens_per_cond):
    t, d = x.shape
    tm = 512
    kern = functools.partial(_norm_mod_kernel, row0=row0, tiles_per_cond=tokens_per_cond // tm)
    return pl.pallas_call(
        kern,
        grid=(t // tm,),
        in_specs=[
            pl.BlockSpec((tm, d), lambda i: (i, 0)),
            pl.BlockSpec((1, d), lambda i: (0, 0)),
            pl.BlockSpec((None, COND_ROWS, 3 * d), lambda i: (layer, 0, 0)),
        ],
        out_specs=pl.BlockSpec((tm, d), lambda i: (i, 0)),
        out_shape=jax.ShapeDtypeStruct((t, d), BF16),
        compiler_params=_cparams(1, 40),
        name="norm_mod",
    )(x, g.reshape(1, d), mod)


def _matmul_kernel(a_ref, b_ref, o_ref):
    o_ref[...] = _dot(a_ref[...], b_ref[...])


def _in_proj(h, w, layer, col0):
    t, k = h.shape
    n = w.shape[2] - col0
    tm, tn = 1024, 1024
    cb0 = col0 // tn
    return pl.pallas_call(
        _matmul_kernel,
        grid=(t // tm, n // tn),
        in_specs=[
            pl.BlockSpec((tm, k), lambda i, j: (i, 0)),
            pl.BlockSpec((None, k, tn), lambda i, j: (layer, 0, cb0 + j)),
        ],
        out_specs=pl.BlockSpec((tm, tn), lambda i, j: (i, j)),
        out_shape=jax.ShapeDtypeStruct((t, n), F32),
        compiler_params=_cparams(2, 52),
        name="in_proj",
    )(h, w)


def _ctx_attn_kernel(q_ref, k_ref, v_ref, g_ref, o_ref):
    n_heads = q_ref.shape[1] // HEAD_DIM_A
    scale = HEAD_DIM_A ** -0.5
    for h in range(n_heads):
        hs = slice(h * HEAD_DIM_A, (h + 1) * HEAD_DIM_A)
        s = _dot_nt(q_ref[:, hs], k_ref[:, hs].astype(BF16)) * scale
        p = jnp.exp(s - jnp.max(s, axis=-1, keepdims=True))
        o = _dot(p.astype(BF16), v_ref[:, hs].astype(BF16)) / jnp.sum(p, axis=-1, keepdims=True)
        o_ref[:, hs] = (o * _silu(g_ref[:, hs])).astype(BF16)


def _ctx_attention(q, cache_k, cache_v, layer, z, batch, seq, w_a):
    t = q.shape[0]
    cache_spec = pl.BlockSpec((None, None, seq, w_a), lambda b: (b, layer, 0, 0))
    tok_spec = pl.BlockSpec((seq, w_a), lambda b: (b, 0))
    return pl.pallas_call(
        _ctx_attn_kernel,
        grid=(batch,),
        in_specs=[tok_spec, cache_spec, cache_spec, tok_spec],
        out_specs=tok_spec,
        out_shape=jax.ShapeDtypeStruct((t, w_a), BF16),
        compiler_params=_cparams(1, 40),
        name="ctx_attention",
    )(q, cache_k, cache_v, z)


def _swap_halves_32(x):
    lane = lax.broadcasted_iota(jnp.int32, x.shape, 1)
    return jnp.where((lane & 32) == 0, pltpu.roll(x, 96, 1), pltpu.roll(x, 32, 1))


def _qkv_kernel(a_ref, wq_ref, wk_ref, wv_ref, qg_ref, kg_ref, *rest, latent, chunk, layer_slot):
    q_out, k_out, v_out = rest[-3:]
    n_heads = wq_ref.shape[1] // HEAD_DIM_A
    if not latent:
        for slot in range(k_out.shape[1]):
            if slot != layer_slot:
                k_out[:, slot] = jnp.zeros_like(k_out[:, slot])
                v_out[:, slot] = jnp.zeros_like(v_out[:, slot])
    for c in range(a_ref.shape[0] // chunk):
        rs = slice(c * chunk, (c + 1) * chunk)
        a = a_ref[rs, :]
        q = _dot(a, wq_ref[...])
        k = _dot(a, wk_ref[...])
        v = _dot(a, wv_ref[...])
        for h in range(n_heads):
            hs = slice(h * HEAD_DIM_A, (h + 1) * HEAD_DIM_A)
            qn = _rms(q[:, hs], qg_ref[...])
            kn = _rms(k[:, hs], kg_ref[...])
            if latent:
                cos = rest[0][rs, :]
                sin = rest[1][rs, :]
                q_out[rs, hs] = (qn * cos + _swap_halves_32(qn) * sin).astype(BF16)
                k_out[rs, hs] = (kn * cos + _swap_halves_32(kn) * sin).astype(BF16)
            else:
                q_out[rs, hs] = qn.astype(BF16)
                k_out[c, layer_slot, :, hs] = kn
        if latent:
            v_out[rs, :] = v.astype(BF16)
        else:
            v_out[c, layer_slot] = v


def _qkv_proj(h, w, layer, qg, kg, w_a, seq, cos=None, sin=None, batch=None, depth=None, caches=None):
    t, kdim = h.shape
    latent = cos is not None
    creates_caches = not latent and caches is None
    tm, tn = (1024, 256) if creates_caches else (1024, 512)
    nb = w_a // tn
    layer_slot = 0
    wspec = lambda seg: pl.BlockSpec((None, kdim, tn), lambda i, j: (layer, 0, seg * nb + j))
    gspec = pl.BlockSpec((1, HEAD_DIM_A), lambda i, j: (0, 0))
    tok_spec = pl.BlockSpec((tm, tn), lambda i, j: (i, j))
    tok_shape = jax.ShapeDtypeStruct((t, w_a), BF16)
    args = [h, w, w, w, qg.reshape(1, -1), kg.reshape(1, -1)]
    in_specs = [pl.BlockSpec((tm, kdim), lambda i, j: (i, 0)), wspec(0), wspec(1), wspec(2), gspec, gspec]
    aliases = {}
    if latent:
        tab = pl.BlockSpec((tm, HEAD_DIM_A), lambda i, j: (i % (seq // tm), 0))
        args += [cos, sin]
        in_specs += [tab, tab]
        out_specs = [tok_spec] * 3
        out_shape = [tok_shape] * 3
    else:
        cache_shape = jax.ShapeDtypeStruct((batch, depth, seq, w_a), F32)
        if creates_caches:
            layer_slot = layer
            cache_spec = pl.BlockSpec((tm // seq, depth, seq, tn), lambda i, j: (i, 0, 0, j))
        else:
            cache_spec = pl.BlockSpec((tm // seq, 1, seq, tn), lambda i, j: (i, layer, 0, j))
            aliases = {len(args): 1, len(args) + 1: 2}
            args += list(caches)
            in_specs += [pl.BlockSpec(memory_space=pl.ANY)] * 2
        out_specs = [tok_spec, cache_spec, cache_spec]
        out_shape = [tok_shape, cache_shape, cache_shape]
    return pl.pallas_call(
        functools.partial(_qkv_kernel, latent=latent, chunk=256 if latent else seq, layer_slot=layer_slot),
        grid=(t // tm, nb),
        in_specs=in_specs,
        out_specs=out_specs,
        out_shape=out_shape,
        input_output_aliases=aliases,
        compiler_params=_cparams(2, 56),
        name="qkv_proj",
    )(*args)


def _rpb_table_kernel(rpb_ref, t_ref, ts_ref, blk_ref, *, kh, shifted_dr):
    n_dr = 2 * WIN_H - 1
    n_dc = 2 * WIN_W - 1
    h = pl.program_id(0)
    shape = (GRID_W, 2 * GRID_W)
    qcol = lax.broadcasted_iota(jnp.int32, shape, 0)
    lane = lax.broadcasted_iota(jnp.int32, shape, 1)
    kcol = lane & (GRID_W - 1)
    dc = jnp.clip(kcol - qcol, 1 - WIN_W, WIN_W - 1) + (WIN_W - 1)
    c0 = jnp.clip(qcol - WIN_W // 2, 0, GRID_W - WIN_W)
    in_win = (kcol >= c0) & (kcol < c0 + WIN_W)
    for dr in range(n_dr):
        acc = jnp.zeros(shape, F32)
        for j in range(n_dc):
            acc = jnp.where(dc == j, rpb_ref[(h * n_dr + dr) * n_dc + j], acc)
        blk_ref[dr] = jnp.where(in_win, acc, NEG_INF)

    left = lane < GRID_W
    neg = jnp.full(shape, NEG_INF, F32)

    def pair(a, b):
        return jnp.where(left, neg if a is None else blk_ref[a], neg if b is None else blk_ref[b])

    for d in range(WIN_H):
        for p in range(kh // 2):
            t_ref[d, 0, :, p * 2 * GRID_W:(p + 1) * 2 * GRID_W] = pair(d + 2 * p, d + 2 * p + 1)
    units = [None] + [shifted_dr + u for u in range(kh)] + [None]
    for p in range(len(units) // 2):
        ts_ref[0, :, p * 2 * GRID_W:(p + 1) * 2 * GRID_W] = pair(units[2 * p], units[2 * p + 1])


def _rpb_tables(rpb_l, kh, shifted_dr):
    n_heads = rpb_l.shape[0]
    assert 2 * GRID_W == 128 and kh % 2 == 0
    return pl.pallas_call(
        functools.partial(_rpb_table_kernel, kh=kh, shifted_dr=shifted_dr),
        grid=(n_heads,),
        in_specs=[pl.BlockSpec(memory_space=pltpu.SMEM)],
        out_specs=[pl.BlockSpec((WIN_H, 1, GRID_W, kh * GRID_W), lambda h: (0, h, 0, 0)),
                   pl.BlockSpec((1, GRID_W, (kh + 2) * GRID_W), lambda h: (h, 0, 0))],
        out_shape=[jax.ShapeDtypeStruct((WIN_H, n_heads, GRID_W, kh * GRID_W), F32),
                   jax.ShapeDtypeStruct((n_heads, GRID_W, (kh + 2) * GRID_W), F32)],
        scratch_shapes=[pltpu.VMEM((2 * WIN_H - 1, GRID_W, 2 * GRID_W), F32)],
        compiler_params=_cparams(1, 32),
        name="rpb_table",
    )(rpb_l.reshape(-1))


NA_QROWS = 4
NA_KROWS = NA_QROWS + WIN_H


def _na_block_plan(rows, kh):
    plans = {}
    for jb in range(rows // NA_QROWS):
        k0 = min(max(NA_QROWS * jb - kh // 2, 0), rows - NA_KROWS)
        plan = []
        for i in range(NA_QROWS):
            r = NA_QROWS * jb + i
            r0 = min(max(r - kh // 2, 0), rows - kh)
            assert 0 <= r0 - k0 <= NA_KROWS - kh
            plan.append((r0 - k0, r0 - r + WIN_H - 1))
        plans.setdefault(tuple(plan), []).append(jb)
    return plans


def _na_kernel(q_ref, k_ref, v_ref, kc_ref, vc_ref, g_ref, t_ref, ts_ref, o_ref, bias_ref, kcb_ref, vcb_ref,
               *, rows, kh, heads_per_step, shifted_dr):
    scale = HEAD_DIM_A ** -0.5
    jb = pl.program_id(2)
    span = kh * GRID_W

    for plan, jbs in _na_block_plan(rows, kh).items():
        pred = functools.reduce(jnp.logical_or, [jb == j for j in jbs if j - 1 not in jbs])

        @pl.when(pred)
        def _():
            for hh in range(heads_per_step):
                for i, (off, dr0) in enumerate(plan):
                    rs = slice(i * GRID_W, (i + 1) * GRID_W)
                    bias_ref[hh, rs, :] = jnp.full((GRID_W, NA_KROWS * GRID_W), NEG_INF, F32)
                    if off % 2 == 0:
                        bias_ref[hh, rs, off * GRID_W:off * GRID_W + span] = t_ref[dr0, hh]
                    else:
                        assert dr0 == shifted_dr
                        lo = (off - 1) * GRID_W
                        bias_ref[hh, rs, lo:lo + span + 2 * GRID_W] = ts_ref[hh]

    @pl.when(jb == 0)
    def _():
        kcb_ref[...] = kc_ref[...].astype(BF16)
        vcb_ref[...] = vc_ref[...].astype(BF16)

    k0 = jnp.clip(NA_QROWS * jb - kh // 2, 0, rows - NA_KROWS)
    ks = pl.ds(pl.multiple_of(k0 * GRID_W, NA_QROWS * GRID_W), NA_KROWS * GRID_W)
    for hh in range(heads_per_step):
        hs = slice(hh * HEAD_DIM_A, (hh + 1) * HEAD_DIM_A)
        q = q_ref[:, hs]
        s_loc = _dot_nt(q, k_ref[ks, hs]) * scale + bias_ref[hh]
        s_ctx = _dot_nt(q, kcb_ref[:, hs]) * scale
        m = jnp.maximum(jnp.max(s_loc, axis=-1, keepdims=True), jnp.max(s_ctx, axis=-1, keepdims=True))
        p_loc = jnp.exp(s_loc - m)
        p_ctx = jnp.exp(s_ctx - m)
        den = jnp.sum(p_loc, axis=-1, keepdims=True) + jnp.sum(p_ctx, axis=-1, keepdims=True)
        o = (_dot(p_loc.astype(BF16), v_ref[ks, hs]) + _dot(p_ctx.astype(BF16), vcb_ref[:, hs])) / den
        o_ref[:, hs] = (o * _silu(g_ref[:, hs])).astype(BF16)


def _nbr_attention(qr, kr, vb, cache_k, cache_v, layer, z, rpb_l, batch, seq, w_a):
    t = qr.shape[0]
    rows = seq // GRID_W
    kh = min(WIN_H, rows)
    assert rows % NA_QROWS == 0 and rows >= NA_KROWS and NA_QROWS % 2 == 0
    heads_per_step = 8
    wb = heads_per_step * HEAD_DIM_A
    tq = NA_QROWS * GRID_W
    steps_per_seq = seq // tq
    lc = cache_k.shape[2]
    ga_col0 = 0
    shifted_dr = WIN_H - 1 - kh // 2
    bias, bias_shifted = _rpb_tables(rpb_l, kh, shifted_dr)
    kern = functools.partial(_na_kernel, rows=rows, kh=kh, heads_per_step=heads_per_step, shifted_dr=shifted_dr)
    qspec = pl.BlockSpec((tq, wb), lambda g, b, r: (b * steps_per_seq + r, g))
    kvspec = pl.BlockSpec((seq, wb), lambda g, b, r: (b, g))
    cspec = pl.BlockSpec((None, None, lc, wb), lambda g, b, r: (b, layer, 0, g))
    return pl.pallas_call(
        kern,
        grid=(w_a // wb, batch, steps_per_seq),
        in_specs=[
            qspec, kvspec, kvspec, cspec, cspec,
            pl.BlockSpec((tq, wb), lambda g, b, r: (b * steps_per_seq + r, ga_col0 + g)),
            pl.BlockSpec((WIN_H, heads_per_step, GRID_W, kh * GRID_W), lambda g, b, r: (0, g, 0, 0),
                         pipeline_mode=pl.Buffered(1)),
            pl.BlockSpec((heads_per_step, GRID_W, (kh + 2) * GRID_W), lambda g, b, r: (g, 0, 0),
                         pipeline_mode=pl.Buffered(1)),
        ],
        out_specs=qspec,
        out_shape=jax.ShapeDtypeStruct((t, w_a), BF16),
        scratch_shapes=[pltpu.VMEM((heads_per_step, tq, NA_KROWS * GRID_W), F32),
                        pltpu.VMEM((lc, wb), BF16), pltpu.VMEM((lc, wb), BF16)],
        compiler_params=_cparams(3, 56),
        name="nbr_attention",
    )(qr, kr, vb, cache_k, cache_v, z, bias, bias_shifted)


def _conv_kernel(u_ref, gl_ref, up_ref, glp_ref, un_ref, gln_ref, gb_ref, cw_ref, cb_ref, lng_ref, lnb_ref,
                 wpw_ref, o_ref, hp_ref, hs_ref, acc_ref, *, tiles_per_seq):
    tl, wb = u_ref.shape
    i = pl.program_id(0)
    first = (i % tiles_per_seq) == 0
    last = (i % tiles_per_seq) == tiles_per_seq - 1
    halo = CONV_HALO
    hp_ref[0:halo, :] = jnp.where(first, 0.0, up_ref[...] * jax.nn.sigmoid(glp_ref[...]))
    hp_ref[halo:halo + tl, :] = u_ref[...] * jax.nn.sigmoid(gl_ref[...])
    hp_ref[halo + tl:2 * halo + tl, :] = jnp.where(last, 0.0, un_ref[...] * jax.nn.sigmoid(gln_ref[...]))
    span = hs_ref.shape[1]
    for s in range(8):
        hs_ref[s] = hp_ref[s:s + span, :]

    rt = 128
    lanes = 128
    off = halo - CONV_K // 2

    for cc in range(wb // lanes):
        cs = slice(cc * lanes, (cc + 1) * lanes)

        def row_body(ri, carry, cs=cs):
            r0 = pl.multiple_of(ri * rt, rt)
            acc = jnp.broadcast_to(cb_ref[:, cs], (rt, lanes))
            for s in range(8):
                taps = [k for k in range(CONV_K) if (k + off) % 8 == s]
                reach = 8 * max((k + off) // 8 for k in taps)
                win = hs_ref[s, pl.ds(r0, rt + reach), cs]
                for k in taps:
                    a = (k + off) // 8
                    acc = acc + win[8 * a:8 * a + rt] * cw_ref[k:k + 1, cs]
            acc_ref[pl.ds(r0, rt), cs] = acc
            return carry

        lax.fori_loop(0, tl // rt, row_body, 0)

    y = acc_ref[...]
    mu = jnp.mean(y, axis=-1, keepdims=True)
    yc = y - mu
    var = jnp.mean(yc * yc, axis=-1, keepdims=True)
    yn = yc * lax.rsqrt(var + EPS) * lng_ref[...] + lnb_ref[...]
    ob = _dot(_silu(yn).astype(BF16), wpw_ref[...])
    o_ref[...] = (ob * _silu(gb_ref[...])).astype(BF16)


def _conformer(z, conv_w, conv_b, ln_g, ln_b, w_pw, layer, seq, w_b, col0):
    t = z.shape[0]
    tl = min(seq, 256)
    tiles_per_seq = seq // tl
    hb = tl // CONV_HALO
    n_hb = t // CONV_HALO
    c_u, c_gl, c_gb = col0 // w_b, col0 // w_b + 1, col0 // w_b + 2
    main = lambda c: pl.BlockSpec((tl, w_b), lambda i: (i, c))
    prev = lambda c: pl.BlockSpec((CONV_HALO, w_b), lambda i: (jnp.maximum(i * hb - 1, 0), c))
    nxt = lambda c: pl.BlockSpec((CONV_HALO, w_b), lambda i: (jnp.minimum((i + 1) * hb, n_hb - 1), c))
    row = pl.BlockSpec((1, w_b), lambda i: (0, 0))
    span = tl + 2 * CONV_HALO - 8
    kern = functools.partial(_conv_kernel, tiles_per_seq=tiles_per_seq)
    return pl.pallas_call(
        kern,
        grid=(t // tl,),
        in_specs=[
            main(c_u), main(c_gl), prev(c_u), prev(c_gl), nxt(c_u), nxt(c_gl), main(c_gb),
            pl.BlockSpec((CONV_K, w_b), lambda i: (0, 0)), row, row, row,
            pl.BlockSpec((None, w_b, w_b), lambda i: (layer, 0, 0)),
        ],
        out_specs=pl.BlockSpec((tl, w_b), lambda i: (i, 0)),
        out_shape=jax.ShapeDtypeStruct((t, w_b), BF16),
        scratch_shapes=[
            pltpu.VMEM((tl + 2 * CONV_HALO, w_b), F32),
            pltpu.VMEM((8, span, w_b), F32),
            pltpu.VMEM((tl, w_b), F32),
        ],
        compiler_params=_cparams(1, 48),
        name="conformer",
    )(z, z, z, z, z, z, z, conv_w, conv_b.reshape(1, -1), ln_g.reshape(1, -1), ln_b.reshape(1, -1), w_pw)


def _swap_halves_128(x):
    half = x.shape[1] // 2
    return jnp.concatenate([x[:, half:], x[:, :half]], axis=1)


def _ret_kernel(coef_ref, q_ref, k_ref, v_ref, g_ref, gn_ref, *rest, n_chunks, n_heads, latent, layer_slot):
    n_scratch = 10
    scratch = rest[-n_scratch:]
    if latent:
        cos_ref, sin_ref, s0_ref, o_ref = rest[:4]
    else:
        o_ref, so_ref = rest[-n_scratch - 2:-n_scratch]
        for slot in range(so_ref.shape[1]):
            if slot != layer_slot:
                so_ref[:, slot] = jnp.zeros_like(so_ref[:, slot])
    m_ref, xf_ref, xb_ref, zf_ref, zb_ref, sf_ref, sb_ref, oacc_ref, qs_ref, ks_ref = scratch
    c = RET_CHUNK
    h = pl.program_id(0)
    lgf = coef_ref[h]
    lgb = coef_ref[n_heads + h]
    dsf = coef_ref[2 * n_heads + h]
    dsb = coef_ref[3 * n_heads + h]

    @pl.when(pl.program_id(1) == 0)
    def _tables():
        rows = lax.broadcasted_iota(jnp.int32, (c, c), 0).astype(F32)
        cols = lax.broadcasted_iota(jnp.int32, (c, c), 1).astype(F32)
        diff = rows - cols
        m_ref[...] = (jnp.where(diff >= 0, jnp.exp(lgf * jnp.maximum(diff, 0.0)), 0.0) * dsf
                      + jnp.where(diff <= 0, jnp.exp(lgb * jnp.maximum(-diff, 0.0)), 0.0) * dsb)
        xf_ref[...] = dsf * jnp.exp(lgf * (rows + 1.0))
        xb_ref[...] = dsb * jnp.exp(lgb * (c - rows))
        zf_ref[...] = jnp.exp(lgf * (c - 1.0 - rows))
        zb_ref[...] = jnp.exp(lgb * rows)

    chunk_len = jnp.full((1, c), float(c), F32)
    gcf = jnp.exp(lgf * chunk_len)
    gcb = jnp.exp(lgb * chunk_len)

    k_scale = HEAD_DIM_C ** -0.5
    if not latent and n_chunks == 1:
        for si in range(q_ref.shape[0] // c):
            rs = slice(si * c, (si + 1) * c)
            k = k_ref[rs, :] * k_scale
            vb = v_ref[rs, :].astype(BF16)
            a = _dot_nt(q_ref[rs, :].astype(BF16), k.astype(BF16))
            o = _dot((a * m_ref[...]).astype(BF16), vb)
            so_ref[si, layer_slot, 0] = _dot((k * zf_ref[...]).T.astype(BF16), vb)
            so_ref[si, layer_slot, 1] = _dot((k * zb_ref[...]).T.astype(BF16), vb)
            o_ref[rs, :] = (_rms(o, gn_ref[...]) * _silu(g_ref[rs, :])).astype(BF16)
        return

    if latent:
        sf_ref[...] = s0_ref[0]
        sb_ref[...] = s0_ref[1]
    else:
        sf_ref[...] = jnp.zeros((c, c), F32)
        sb_ref[...] = jnp.zeros((c, c), F32)
    use_cross = latent or n_chunks > 1

    def forward(ci, carry):
        sl = pl.ds(pl.multiple_of(ci * c, c), c)
        q = q_ref[sl, :]
        k = k_ref[sl, :] * k_scale
        if latent:
            cos = cos_ref[sl, :]
            sin = sin_ref[sl, :]
            q = q * cos + _swap_halves_128(q) * sin
            k = k * cos + _swap_halves_128(k) * sin
        qb = q.astype(BF16)
        vb = v_ref[sl, :].astype(BF16)
        qs_ref[sl, :] = qb
        ks_ref[sl, :] = k
        a = _dot_nt(qb, k.astype(BF16))
        o = _dot((a * m_ref[...]).astype(BF16), vb)
        if use_cross:
            o = o + _dot(qb, sf_ref[...].astype(BF16)) * xf_ref[...]
        oacc_ref[sl, :] = o
        sf_ref[...] = gcf * sf_ref[...] + _dot((k * zf_ref[...]).T.astype(BF16), vb)
        return carry

    def backward(cj, carry):
        ci = n_chunks - 1 - cj
        sl = pl.ds(pl.multiple_of(ci * c, c), c)
        k = ks_ref[sl, :]
        vb = v_ref[sl, :].astype(BF16)
        o = oacc_ref[sl, :]
        if use_cross:
            o = o + _dot(qs_ref[sl, :], sb_ref[...].astype(BF16)) * xb_ref[...]
        sb_ref[...] = gcb * sb_ref[...] + _dot((k * zb_ref[...]).T.astype(BF16), vb)
        o_ref[sl, :] = (_rms(o, gn_ref[...]) * _silu(g_ref[sl, :])).astype(BF16)
        return carry

    lax.fori_loop(0, n_chunks, forward, 0)
    lax.fori_loop(0, n_chunks, backward, 0)
    if not latent:
        so_ref[0, layer_slot, 0] = sf_ref[...]
        so_ref[0, layer_slot, 1] = sb_ref[...]


def _retention(z, coef, gn_g, batch, seq, n_heads, col0, layer, depth=None, cos=None, sin=None, state=None,
               prev_states=None):
    t = z.shape[0]
    dh = HEAD_DIM_C
    latent = state is not None
    n_chunks = seq // RET_CHUNK
    cb = col0 // dh
    seqs = 4 if (not latent and n_chunks == 1 and batch % 4 == 0) else 1
    rows = seqs * seq
    zspec = lambda seg: pl.BlockSpec((rows, dh), lambda h, b: (b, cb + seg * n_heads + h))
    in_specs = [
        pl.BlockSpec(memory_space=pltpu.SMEM),
        zspec(0), zspec(1), zspec(2), zspec(3),
        pl.BlockSpec((1, dh), lambda h, b: (0, h)),
    ]
    args = [coef, z, z, z, z, gn_g.reshape(1, -1)]
    ospec = pl.BlockSpec((rows, dh), lambda h, b: (b, h))
    oshape = jax.ShapeDtypeStruct((t, n_heads * dh), BF16)
    aliases = {}
    layer_slot = 0
    if latent:
        tab = pl.BlockSpec((seq, dh), lambda h, b: (0, 0))
        state_spec = pl.BlockSpec((None, None, 2, None, dh, dh), lambda h, b: (b, layer, 0, h, 0, 0))
        in_specs += [tab, tab, state_spec]
        args += [cos, sin, state]
        out_specs, out_shape = ospec, oshape
    else:
        if prev_states is None:
            layer_slot = layer
            state_spec = pl.BlockSpec((seqs, depth, 2, None, dh, dh), lambda h, b: (b, 0, 0, h, 0, 0))
        else:
            state_spec = pl.BlockSpec((seqs, 1, 2, None, dh, dh), lambda h, b: (b, layer, 0, h, 0, 0))
            aliases = {len(args): 1}
            args.append(prev_states)
            in_specs.append(pl.BlockSpec(memory_space=pl.ANY))
        out_specs = [ospec, state_spec]
        out_shape = [oshape, jax.ShapeDtypeStruct((batch, depth, 2, n_heads, dh, dh), F32)]
    sq = lambda: pltpu.VMEM((RET_CHUNK, RET_CHUNK), F32)
    kern = functools.partial(_ret_kernel, n_chunks=n_chunks, n_heads=n_heads, latent=latent, layer_slot=layer_slot)
    return pl.pallas_call(
        kern,
        grid=(n_heads, batch // seqs),
        in_specs=in_specs,
        out_specs=out_specs,
        out_shape=out_shape,
        input_output_aliases=aliases,
        scratch_shapes=[sq(), sq(), sq(), sq(), sq(), sq(), sq(),
                        pltpu.VMEM((seq, dh), F32), pltpu.VMEM((seq, dh), BF16), pltpu.VMEM((seq, dh), F32)],
        compiler_params=_cparams(2, 48),
        name="retention",
    )(*args)


def _out_proj_kernel(oa_ref, ob_ref, oc_ref, w_ref, x_ref, m_ref, y_ref, *, row0, tiles_per_cond):
    wa = oa_ref.shape[1]
    wb = ob_ref.shape[1]
    row = row0 + pl.program_id(0) // tiles_per_cond
    acc = _dot(oa_ref[...], w_ref[0:wa, :])
    acc = acc + _dot(ob_ref[...], w_ref[wa:wa + wb, :])
    acc = acc + _dot(oc_ref[...], w_ref[wa + wb:, :])
    y_ref[...] = x_ref[...] + m_ref[pl.ds(row, 1), :] * acc


def _out_proj(oa, ob, oc, w, x, mod, layer, row0, tokens_per_cond):
    t, d = x.shape
    tm, tn = 1024, 1024
    gate_col0 = (2 * d) // tn
    kern = functools.partial(_out_proj_kernel, row0=row0, tiles_per_cond=tokens_per_cond // tm)
    act = lambda a: pl.BlockSpec((tm, a.shape[1]), lambda i, j: (i, 0))
    return pl.pallas_call(
        kern,
        grid=(t // tm, d // tn),
        in_specs=[
            act(oa), act(ob), act(oc),
            pl.BlockSpec((None, d, tn), lambda i, j: (layer, 0, j)),
            pl.BlockSpec((tm, tn), lambda i, j: (i, j)),
            pl.BlockSpec((None, COND_ROWS, tn), lambda i, j: (layer, 0, gate_col0 + j)),
        ],
        out_specs=pl.BlockSpec((tm, tn), lambda i, j: (i, j)),
        out_shape=jax.ShapeDtypeStruct((t, d), F32),
        compiler_params=_cparams(2, 56),
        name="out_proj",
    )(oa, ob, oc, w, x, mod)


def _rope_angles(pos, n_dims):
    half = n_dims // 2
    freqs = ROPE_BASE ** (-jnp.arange(half, dtype=F32) / half)
    return pos.astype(F32)[:, None] * freqs[None, :]


def _axial_tables(seq):
    t = jnp.arange(seq)
    ar = _rope_angles(t // GRID_W, HEAD_DIM_A // 2)
    ac = _rope_angles(t % GRID_W, HEAD_DIM_A // 2)
    cos = jnp.concatenate([jnp.cos(ar), jnp.cos(ar), jnp.cos(ac), jnp.cos(ac)], -1)
    sin = jnp.concatenate([-jnp.sin(ar), jnp.sin(ar), -jnp.sin(ac), jnp.sin(ac)], -1)
    return cos, sin


def _ret_tables(seq):
    ang = _rope_angles(jnp.arange(seq), HEAD_DIM_C)
    cos = jnp.concatenate([jnp.cos(ang), jnp.cos(ang)], -1)
    sin = jnp.concatenate([-jnp.sin(ang), jnp.sin(ang)], -1)
    return cos, sin


def kernel(x_prompt, x_sample, cache_k, cache_v, state_ret, c, c_ctx, norm_g, w_mod, b_mod, w_in, qn_g, kn_g,
           rpb, conv_w, conv_b, cln_g, cln_b, w_pw, ret_decay_logit, ret_dir_scale, ret_gn_g, w_out):
    batch, seq, d = x_prompt.shape
    dec_batch, dec_seq, _ = x_sample.shape
    depth = w_in.shape[0]
    n_heads_a = cache_k.shape[3]
    w_a = n_heads_a * HEAD_DIM_A
    w_b = w_pw.shape[1]
    n_heads_c = state_ret.shape[3]
    w_c = n_heads_c * HEAD_DIM_C
    col_b = w_a
    col_c = col_b + 3 * w_b
    assert w_in.shape[2] == 3 * w_a + col_c + 4 * w_c and w_a + w_b + w_c == d
    assert 1 + dec_batch <= COND_ROWS
    t_ctx = batch * seq
    t_lat = dec_batch * dec_seq

    cond = jnp.concatenate([c_ctx[None], c, jnp.zeros((COND_ROWS - 1 - dec_batch, d), F32)], axis=0)
    mod = _modulation(cond, w_mod, b_mod)
    w_in_b = w_in.astype(BF16)
    w_out_b = w_out.astype(BF16)
    w_pw_b = w_pw.astype(BF16)
    cache_k2 = cache_k.reshape(dec_batch, depth, -1, w_a)
    cache_v2 = cache_v.reshape(dec_batch, depth, -1, w_a)
    ax_cos, ax_sin = _axial_tables(dec_seq)
    rt_cos, rt_sin = _ret_tables(dec_seq)
    log_gamma = jax.nn.log_sigmoid(ret_decay_logit.astype(F32))
    ret_coef = jnp.concatenate([log_gamma, ret_dir_scale.astype(F32)], axis=1).reshape(depth, -1)

    xp = x_prompt.reshape(t_ctx, d)
    xs = x_sample.reshape(t_lat, d)
    caches = None
    states = None
    for l in range(depth):
        conv_args = (conv_w[l], conv_b[l], cln_g[l], cln_b[l], w_pw_b, l)
        h = _norm_mod(xp, norm_g[l], mod, l, 0, t_ctx)
        q, *caches = _qkv_proj(h, w_in_b, l, qn_g[l], kn_g[l], w_a, seq, batch=batch, depth=depth, caches=caches)
        z = _in_proj(h, w_in_b, l, 3 * w_a)
        oa = _ctx_attention(q, *caches, l, z, batch, seq, w_a)
        ob = _conformer(z, *conv_args, seq, w_b, col_b)
        oc, states = _retention(z, ret_coef[l], ret_gn_g[l], batch, seq, n_heads_c, col_c, l, depth=depth,
                                prev_states=states)
        xp = _out_proj(oa, ob, oc, w_out_b, xp, mod, l, 0, t_ctx)
        h = _norm_mod(xs, norm_g[l], mod, l, 1, dec_seq)
        qr, kr, vb = _qkv_proj(h, w_in_b, l, qn_g[l], kn_g[l], w_a, dec_seq, cos=ax_cos, sin=ax_sin)
        z = _in_proj(h, w_in_b, l, 3 * w_a)
        oa = _nbr_attention(qr, kr, vb, cache_k2, cache_v2, l, z, rpb[l], dec_batch, dec_seq, w_a)
        ob = _conformer(z, *conv_args, dec_seq, w_b, col_b)
        oc = _retention(z, ret_coef[l], ret_gn_g[l], dec_batch, dec_seq, n_heads_c, col_c, l,
                        cos=rt_cos, sin=rt_sin, state=state_ret)
        xs = _out_proj(oa, ob, oc, w_out_b, xs, mod, l, 1, dec_seq)
    new_k, new_v = (a.reshape(batch, depth, seq, n_heads_a, HEAD_DIM_A) for a in caches)
    return (xp.reshape(batch, seq, d), xs.reshape(dec_batch, dec_seq, d), new_k, new_v, states)
```

```python
import functools

import jax
import jax.numpy as jnp
from jax import lax
from jax.experimental import pallas as pl
from jax.experimental.pallas import tpu as pltpu

F32 = jnp.float32
BF16 = jnp.bfloat16

GRID_W = 64
WIN_H = 8
WIN_W = 16
CONV_K = 31
HEAD_DIM_A = 128
HEAD_DIM_C = 256
ROPE_BASE = 10000.0
EPS = 1e-6
NEG_INF = -1e30

RET_CHUNK = 256
CONV_HALO = 16
COND_ROWS = 8
MIB = 1024 * 1024


def _cparams(n_axes, vmem_mib):
    return pltpu.CompilerParams(
        dimension_semantics=("arbitrary",) * n_axes,
        vmem_limit_bytes=vmem_mib * MIB,
    )


def _silu(x):
    return x * jax.nn.sigmoid(x)


def _dot(a, b):
    return jnp.dot(a, b, preferred_element_type=F32)


def _dot_nt(a, b):
    return lax.dot_general(a, b, (((1,), (1,)), ((), ())), preferred_element_type=F32)


def _rms(x, g):
    return x * lax.rsqrt(jnp.mean(x * x, axis=-1, keepdims=True) + EPS) * g


def _mod_kernel(cond_ref, w_ref, b_ref, o_ref):
    s = _silu(cond_ref[...]).astype(BF16)
    o_ref[0] = _dot(s, w_ref[0].astype(BF16)) + b_ref[0]


def _modulation(cond, w_mod, b_mod):
    depth, d, n = w_mod.shape
    tn = 512
    return pl.pallas_call(
        _mod_kernel,
        grid=(depth, n // tn),
        in_specs=[
            pl.BlockSpec((COND_ROWS, d), lambda l, j: (0, 0)),
            pl.BlockSpec((1, d, tn), lambda l, j: (l, 0, j)),
            pl.BlockSpec((1, 1, tn), lambda l, j: (l, 0, j)),
        ],
        out_specs=pl.BlockSpec((1, COND_ROWS, tn), lambda l, j: (l, 0, j)),
        out_shape=jax.ShapeDtypeStruct((depth, COND_ROWS, n), F32),
        compiler_params=_cparams(2, 40),
        name="modulation",
    )(cond, w_mod, b_mod.reshape(depth, 1, n))


def _norm_mod_kernel(x_ref, g_ref, m_ref, o_ref, *, row0, tiles_per_cond):
    d = x_ref.shape[1]
    row = row0 + pl.program_id(0) // tiles_per_cond
    shift = m_ref[pl.ds(row, 1), 0:d]
    scale = m_ref[pl.ds(row, 1), d:2 * d]
    y = _rms(x_ref[...], g_ref[...])
    o_ref[...] = (y * (1.0 + scale) + shift).astype(BF16)


def _norm_mod(x, g, mod, layer, row0, tokens_per_cond):
    t, d = x.shape
    tm = 512
    kern = functools.partial(_norm_mod_kernel, row0=row0, tiles_per_cond=tokens_per_cond // tm)
    return pl.pallas_call(
        kern,
        grid=(t // tm,),
        in_specs=[
            pl.BlockSpec((tm, d), lambda i: (i, 0)),
            pl.BlockSpec((1, d), lambda i: (0, 0)),
            pl.BlockSpec((None, COND_ROWS, 3 * d), lambda i: (layer, 0, 0)),
        ],
        out_specs=pl.BlockSpec((tm, d), lambda i: (i, 0)),
        out_shape=jax.ShapeDtypeStruct((t, d), BF16),
        compiler_params=_cparams(1, 40),
        name="norm_mod",
    )(x, g.reshape(1, d), mod)


def _matmul_kernel(a_ref, b_ref, o_ref):
    o_ref[...] = _dot(a_ref[...], b_ref[...])


def _in_proj(h, w, layer, col0):
    t, k = h.shape
    n = w.shape[2] - col0
    tm, tn = 1024, 1024
    cb0 = col0 // tn
    return pl.pallas_call(
        _matmul_kernel,
        grid=(t // tm, n // tn),
        in_specs=[
            pl.BlockSpec((tm, k), lambda i, j: (i, 0)),
            pl.BlockSpec((None, k, tn), lambda i, j: (layer, 0, cb0 + j)),
        ],
        out_specs=pl.BlockSpec((tm, tn), lambda i, j: (i, j)),
        out_shape=jax.ShapeDtypeStruct((t, n), F32),
        compiler_params=_cparams(2, 52),
        name="in_proj",
    )(h, w)


def _ctx_attn_kernel(q_ref, k_ref, v_ref, g_ref, o_ref):
    n_heads = q_ref.shape[1] // HEAD_DIM_A
    scale = HEAD_DIM_A ** -0.5
    for h in range(n_heads):
        hs = slice(h * HEAD_DIM_A, (h + 1) * HEAD_DIM_A)
        s = _dot_nt(q_ref[:, hs], k_ref[:, hs].astype(BF16)) * scale
        p = jnp.exp(s - jnp.max(s, axis=-1, keepdims=True))
        o = _dot(p.astype(BF16), v_ref[:, hs].astype(BF16)) / jnp.sum(p, axis=-1, keepdims=True)
        o_ref[:, hs] = (o * _silu(g_ref[:, hs])).astype(BF16)


def _ctx_attention(q, cache_k, cache_v, layer, z, batch, seq, w_a):
    t = q.shape[0]
    cache_spec = pl.BlockSpec((None, None, seq, w_a), lambda b: (b, layer, 0, 0))
    tok_spec = pl.BlockSpec((seq, w_a), lambda b: (b, 0))
    return pl.pallas_call(
        _ctx_attn_kernel,
        grid=(batch,),
        in_specs=[tok_spec, cache_spec, cache_spec, tok_spec],
        out_specs=tok_spec,
        out_shape=jax.ShapeDtypeStruct((t, w_a), BF16),
        compiler_params=_cparams(1, 40),
        name="ctx_attention",
    )(q, cache_k, cache_v, z)


def _swap_halves_32(x):
    lane = lax.broadcasted_iota(jnp.int32, x.shape, 1)
    return jnp.where((lane & 32) == 0, pltpu.roll(x, 96, 1), pltpu.roll(x, 32, 1))


def _qkv_kernel(a_ref, wq_ref, wk_ref, wv_ref, qg_ref, kg_ref, *rest, latent, chunk, layer_slot):
    q_out, k_out, v_out = rest[-3:]
    n_heads = wq_ref.shape[1] // HEAD_DIM_A
    if not latent:
        for slot in range(k_out.shape[1]):
            if slot != layer_slot:
                k_out[:, slot] = jnp.zeros_like(k_out[:, slot])
                v_out[:, slot] = jnp.zeros_like(v_out[:, slot])
    for c in range(a_ref.shape[0] // chunk):
        rs = slice(c * chunk, (c + 1) * chunk)
        a = a_ref[rs, :]
        q = _dot(a, wq_ref[...])
        k = _dot(a, wk_ref[...])
        v = _dot(a, wv_ref[...])
        for h in range(n_heads):
            hs = slice(h * HEAD_DIM_A, (h + 1) * HEAD_DIM_A)
            qn = _rms(q[:, hs], qg_ref[...])
            kn = _rms(k[:, hs], kg_ref[...])
            if latent:
                cos = rest[0][rs, :]
                sin = rest[1][rs, :]
                q_out[rs, hs] = (qn * cos + _swap_halves_32(qn) * sin).astype(BF16)
                k_out[rs, hs] = (kn * cos + _swap_halves_32(kn) * sin).astype(BF16)
            else:
                q_out[rs, hs] = qn.astype(BF16)
                k_out[c, layer_slot, :, hs] = kn
        if latent:
            v_out[rs, :] = v.astype(BF16)
        else:
            v_out[c, layer_slot] = v


def _qkv_proj(h, w, layer, qg, kg, w_a, seq, cos=None, sin=None, batch=None, depth=None, caches=None):
    t, kdim = h.shape
    latent = cos is not None
    creates_caches = not latent and caches is None
    tm, tn = (1024, 256) if creates_caches else (1024, 512)
    nb = w_a // tn
    layer_slot = 0
    wspec = lambda seg: pl.BlockSpec((None, kdim, tn), lambda i, j: (layer, 0, seg * nb + j))
    gspec = pl.BlockSpec((1, HEAD_DIM_A), lambda i, j: (0, 0))
    tok_spec = pl.BlockSpec((tm, tn), lambda i, j: (i, j))
    tok_shape = jax.ShapeDtypeStruct((t, w_a), BF16)
    args = [h, w, w, w, qg.reshape(1, -1), kg.reshape(1, -1)]
    in_specs = [pl.BlockSpec((tm, kdim), lambda i, j: (i, 0)), wspec(0), wspec(1), wspec(2), gspec, gspec]
    aliases = {}
    if latent:
        tab = pl.BlockSpec((tm, HEAD_DIM_A), lambda i, j: (i % (seq // tm), 0))
        args += [cos, sin]
        in_specs += [tab, tab]
        out_specs = [tok_spec] * 3
        out_shape = [tok_shape] * 3
    else:
        cache_shape = jax.ShapeDtypeStruct((batch, depth, seq, w_a), F32)
        if creates_caches:
            layer_slot = layer
            cache_spec = pl.BlockSpec((tm // seq, depth, seq, tn), lambda i, j: (i, 0, 0, j))
        else:
            cache_spec = pl.BlockSpec((tm // seq, 1, seq, tn), lambda i, j: (i, layer, 0, j))
            aliases = {len(args): 1, len(args) + 1: 2}
            args += list(caches)
            in_specs += [pl.BlockSpec(memory_space=pl.ANY)] * 2
        out_specs = [tok_spec, cache_spec, cache_spec]
        out_shape = [tok_shape, cache_shape, cache_shape]
    return pl.pallas_call(
        functools.partial(_qkv_kernel, latent=latent, chunk=256 if latent else seq, layer_slot=layer_slot),
        grid=(t // tm, nb),
        in_specs=in_specs,
        out_specs=out_specs,
        out_shape=out_shape,
        input_output_aliases=aliases,
        compiler_params=_cparams(2, 56),
        name="qkv_proj",
    )(*args)


def _rpb_table_kernel(rpb_ref, t_ref, ts_ref, blk_ref, *, kh, shifted_dr):
    n_dr = 2 * WIN_H - 1
    n_dc = 2 * WIN_W - 1
    h = pl.program_id(0)
    shape = (GRID_W, 2 * GRID_W)
    qcol = lax.broadcasted_iota(jnp.int32, shape, 0)
    lane = lax.broadcasted_iota(jnp.int32, shape, 1)
    kcol = lane & (GRID_W - 1)
    dc = jnp.clip(kcol - qcol, 1 - WIN_W, WIN_W - 1) + (WIN_W - 1)
    c0 = jnp.clip(qcol - WIN_W // 2, 0, GRID_W - WIN_W)
    in_win = (kcol >= c0) & (kcol < c0 + WIN_W)
    for dr in range(n_dr):
        acc = jnp.zeros(shape, F32)
        for j in range(n_dc):
            acc = jnp.where(dc == j, rpb_ref[(h * n_dr + dr) * n_dc + j], acc)
        blk_ref[dr] = jnp.where(in_win, acc, NEG_INF)

    left = lane < GRID_W
    neg = jnp.full(shape, NEG_INF, F32)

    def pair(a, b):
        return jnp.where(left, neg if a is None else blk_ref[a], neg if b is None else blk_ref[b])

    for d in range(WIN_H):
        for p in range(kh // 2):
            t_ref[d, 0, :, p * 2 * GRID_W:(p + 1) * 2 * GRID_W] = pair(d + 2 * p, d + 2 * p + 1)
    units = [None] + [shifted_dr + u for u in range(kh)] + [None]
    for p in range(len(units) // 2):
        ts_ref[0, :, p * 2 * GRID_W:(p + 1) * 2 * GRID_W] = pair(units[2 * p], units[2 * p + 1])


def _rpb_tables(rpb_l, kh, shifted_dr):
    n_heads = rpb_l.shape[0]
    assert 2 * GRID_W == 128 and kh % 2 == 0
    return pl.pallas_call(
        functools.partial(_rpb_table_kernel, kh=kh, shifted_dr=shifted_dr),
        grid=(n_heads,),
        in_specs=[pl.BlockSpec(memory_space=pltpu.SMEM)],
        out_specs=[pl.BlockSpec((WIN_H, 1, GRID_W, kh * GRID_W), lambda h: (0, h, 0, 0)),
                   pl.BlockSpec((1, GRID_W, (kh + 2) * GRID_W), lambda h: (h, 0, 0))],
        out_shape=[jax.ShapeDtypeStruct((WIN_H, n_heads, GRID_W, kh * GRID_W), F32),
                   jax.ShapeDtypeStruct((n_heads, GRID_W, (kh + 2) * GRID_W), F32)],
        scratch_shapes=[pltpu.VMEM((2 * WIN_H - 1, GRID_W, 2 * GRID_W), F32)],
        compiler_params=_cparams(1, 32),
        name="rpb_table",
    )(rpb_l.reshape(-1))


NA_QROWS = 4
NA_KROWS = NA_QROWS + WIN_H


def _na_block_plan(rows, kh):
    plans = {}
    for jb in range(rows // NA_QROWS):
        k0 = min(max(NA_QROWS * jb - kh // 2, 0), rows - NA_KROWS)
        plan = []
        for i in range(NA_QROWS):
            r = NA_QROWS * jb + i
            r0 = min(max(r - kh // 2, 0), rows - kh)
            assert 0 <= r0 - k0 <= NA_KROWS - kh
            plan.append((r0 - k0, r0 - r + WIN_H - 1))
        plans.setdefault(tuple(plan), []).append(jb)
    return plans


def _na_kernel(q_ref, k_ref, v_ref, kc_ref, vc_ref, g_ref, t_ref, ts_ref, o_ref, bias_ref, kcb_ref, vcb_ref,
               *, rows, kh, heads_per_step, shifted_dr):
    scale = HEAD_DIM_A ** -0.5
    jb = pl.program_id(2)
    span = kh * GRID_W

    for plan, jbs in _na_block_plan(rows, kh).items():
        pred = functools.reduce(jnp.logical_or, [jb == j for j in jbs if j - 1 not in jbs])

        @pl.when(pred)
        def _():
            for hh in range(heads_per_step):
                for i, (off, dr0) in enumerate(plan):
                    rs = slice(i * GRID_W, (i + 1) * GRID_W)
                    bias_ref[hh, rs, :] = jnp.full((GRID_W, NA_KROWS * GRID_W), NEG_INF, F32)
                    if off % 2 == 0:
                        bias_ref[hh, rs, off * GRID_W:off * GRID_W + span] = t_ref[dr0, hh]
                    else:
                        assert dr0 == shifted_dr
                        lo = (off - 1) * GRID_W
                        bias_ref[hh, rs, lo:lo + span + 2 * GRID_W] = ts_ref[hh]

    @pl.when(jb == 0)
    def _():
        for hh in range(heads_per_step):
            hs = slice(hh * HEAD_DIM_A, (hh + 1) * HEAD_DIM_A)
            kcb_ref[:, hs] = kc_ref[:, hh, :].astype(BF16)
            vcb_ref[:, hs] = vc_ref[:, hh, :].astype(BF16)

    k0 = jnp.clip(NA_QROWS * jb - kh // 2, 0, rows - NA_KROWS)
    ks = pl.ds(pl.multiple_of(k0 * GRID_W, NA_QROWS * GRID_W), NA_KROWS * GRID_W)
    for hh in range(heads_per_step):
        hs = slice(hh * HEAD_DIM_A, (hh + 1) * HEAD_DIM_A)
        q = q_ref[:, hs]
        s_loc = _dot_nt(q, k_ref[ks, hs]) * scale + bias_ref[hh]
        s_ctx = _dot_nt(q, kcb_ref[:, hs]) * scale
        m = jnp.maximum(jnp.max(s_loc, axis=-1, keepdims=True), jnp.max(s_ctx, axis=-1, keepdims=True))
        p_loc = jnp.exp(s_loc - m)
        p_ctx = jnp.exp(s_ctx - m)
        den = jnp.sum(p_loc, axis=-1, keepdims=True) + jnp.sum(p_ctx, axis=-1, keepdims=True)
        o = (_dot(p_loc.astype(BF16), v_ref[ks, hs]) + _dot(p_ctx.astype(BF16), vcb_ref[:, hs])) / den
        o_ref[:, hs] = (o * _silu(g_ref[:, hs])).astype(BF16)


def _nbr_attention(qr, kr, vb, cache_k, cache_v, layer, z, rpb_l, batch, seq, w_a):
    t = qr.shape[0]
    rows = seq // GRID_W
    kh = min(WIN_H, rows)
    assert rows % NA_QROWS == 0 and rows >= NA_KROWS and NA_QROWS % 2 == 0
    heads_per_step = 8
    wb = heads_per_step * HEAD_DIM_A
    tq = NA_QROWS * GRID_W
    steps_per_seq = seq // tq
    lc = cache_k.shape[2]
    ga_col0 = 0
    shifted_dr = WIN_H - 1 - kh // 2
    bias, bias_shifted = _rpb_tables(rpb_l, kh, shifted_dr)
    kern = functools.partial(_na_kernel, rows=rows, kh=kh, heads_per_step=heads_per_step, shifted_dr=shifted_dr)
    qspec = pl.BlockSpec((tq, wb), lambda g, b, r: (b * steps_per_seq + r, g))
    kvspec = pl.BlockSpec((seq, wb), lambda g, b, r: (b, g))
    cspec = pl.BlockSpec((None, None, lc, heads_per_step, HEAD_DIM_A), lambda g, b, r: (b, layer, 0, g, 0))
    return pl.pallas_call(
        kern,
        grid=(w_a // wb, batch, steps_per_seq),
        in_specs=[
            qspec, kvspec, kvspec, cspec, cspec,
            pl.BlockSpec((tq, wb), lambda g, b, r: (b * steps_per_seq + r, ga_col0 + g)),
            pl.BlockSpec((WIN_H, heads_per_step, GRID_W, kh * GRID_W), lambda g, b, r: (0, g, 0, 0),
                         pipeline_mode=pl.Buffered(1)),
            pl.BlockSpec((heads_per_step, GRID_W, (kh + 2) * GRID_W), lambda g, b, r: (g, 0, 0),
                         pipeline_mode=pl.Buffered(1)),
        ],
        out_specs=qspec,
        out_shape=jax.ShapeDtypeStruct((t, w_a), BF16),
        scratch_shapes=[pltpu.VMEM((heads_per_step, tq, NA_KROWS * GRID_W), F32),
                        pltpu.VMEM((lc, wb), BF16), pltpu.VMEM((lc, wb), BF16)],
        compiler_params=_cparams(3, 56),
        name="nbr_attention",
    )(qr, kr, vb, cache_k, cache_v, z, bias, bias_shifted)


def _conv_kernel(u_ref, gl_ref, up_ref, glp_ref, un_ref, gln_ref, gb_ref, cw_ref, cb_ref, lng_ref, lnb_ref,
                 wpw_ref, o_ref, hp_ref, hs_ref, acc_ref, *, tiles_per_seq):
    tl, wb = u_ref.shape
    i = pl.program_id(0)
    first = (i % tiles_per_seq) == 0
    last = (i % tiles_per_seq) == tiles_per_seq - 1
    halo = CONV_HALO
    hp_ref[0:halo, :] = jnp.where(first, 0.0, up_ref[...] * jax.nn.sigmoid(glp_ref[...]))
    hp_ref[halo:halo + tl, :] = u_ref[...] * jax.nn.sigmoid(gl_ref[...])
    hp_ref[halo + tl:2 * halo + tl, :] = jnp.where(last, 0.0, un_ref[...] * jax.nn.sigmoid(gln_ref[...]))
    span = hs_ref.shape[1]
    for s in range(8):
        hs_ref[s] = hp_ref[s:s + span, :]

    rt = 128
    lanes = 128
    off = halo - CONV_K // 2

    for cc in range(wb // lanes):
        cs = slice(cc * lanes, (cc + 1) * lanes)

        def row_body(ri, carry, cs=cs):
            r0 = pl.multiple_of(ri * rt, rt)
            acc = jnp.broadcast_to(cb_ref[:, cs], (rt, lanes))
            for s in range(8):
                taps = [k for k in range(CONV_K) if (k + off) % 8 == s]
                reach = 8 * max((k + off) // 8 for k in taps)
                win = hs_ref[s, pl.ds(r0, rt + reach), cs]
                for k in taps:
                    a = (k + off) // 8
                    acc = acc + win[8 * a:8 * a + rt] * cw_ref[k:k + 1, cs]
            acc_ref[pl.ds(r0, rt), cs] = acc
            return carry

        lax.fori_loop(0, tl // rt, row_body, 0)

    y = acc_ref[...]
    mu = jnp.mean(y, axis=-1, keepdims=True)
    yc = y - mu
    var = jnp.mean(yc * yc, axis=-1, keepdims=True)
    yn = yc * lax.rsqrt(var + EPS) * lng_ref[...] + lnb_ref[...]
    ob = _dot(_silu(yn).astype(BF16), wpw_ref[...])
    o_ref[...] = (ob * _silu(gb_ref[...])).astype(BF16)


def _conformer(z, conv_w, conv_b, ln_g, ln_b, w_pw, layer, seq, w_b, col0):
    t = z.shape[0]
    tl = min(seq, 256)
    tiles_per_seq = seq // tl
    hb = tl // CONV_HALO
    n_hb = t // CONV_HALO
    c_u, c_gl, c_gb = col0 // w_b, col0 // w_b + 1, col0 // w_b + 2
    main = lambda c: pl.BlockSpec((tl, w_b), lambda i: (i, c))
    prev = lambda c: pl.BlockSpec((CONV_HALO, w_b), lambda i: (jnp.maximum(i * hb - 1, 0), c))
    nxt = lambda c: pl.BlockSpec((CONV_HALO, w_b), lambda i: (jnp.minimum((i + 1) * hb, n_hb - 1), c))
    row = pl.BlockSpec((1, w_b), lambda i: (0, 0))
    span = tl + 2 * CONV_HALO - 8
    kern = functools.partial(_conv_kernel, tiles_per_seq=tiles_per_seq)
    return pl.pallas_call(
        kern,
        grid=(t // tl,),
        in_specs=[
            main(c_u), main(c_gl), prev(c_u), prev(c_gl), nxt(c_u), nxt(c_gl), main(c_gb),
            pl.BlockSpec((CONV_K, w_b), lambda i: (0, 0)), row, row, row,
            pl.BlockSpec((None, w_b, w_b), lambda i: (layer, 0, 0)),
        ],
        out_specs=pl.BlockSpec((tl, w_b), lambda i: (i, 0)),
        out_shape=jax.ShapeDtypeStruct((t, w_b), BF16),
        scratch_shapes=[
            pltpu.VMEM((tl + 2 * CONV_HALO, w_b), F32),
            pltpu.VMEM((8, span, w_b), F32),
            pltpu.VMEM((tl, w_b), F32),
        ],
        compiler_params=_cparams(1, 48),
        name="conformer",
    )(z, z, z, z, z, z, z, conv_w, conv_b.reshape(1, -1), ln_g.reshape(1, -1), ln_b.reshape(1, -1), w_pw)


def _swap_halves_128(x):
    half = x.shape[1] // 2
    return jnp.concatenate([x[:, half:], x[:, :half]], axis=1)


def _ret_kernel(coef_ref, q_ref, k_ref, v_ref, g_ref, gn_ref, *rest, n_chunks, n_heads, latent, layer_slot):
    n_scratch = 10
    scratch = rest[-n_scratch:]
    if latent:
        cos_ref, sin_ref, s0_ref, o_ref = rest[:4]
    else:
        o_ref, so_ref = rest[-n_scratch - 2:-n_scratch]
        for slot in range(so_ref.shape[1]):
            if slot != layer_slot:
                so_ref[:, slot] = jnp.zeros_like(so_ref[:, slot])
    m_ref, xf_ref, xb_ref, zf_ref, zb_ref, sf_ref, sb_ref, oacc_ref, qs_ref, ks_ref = scratch
    c = RET_CHUNK
    h = pl.program_id(0)
    lgf = coef_ref[h]
    lgb = coef_ref[n_heads + h]
    dsf = coef_ref[2 * n_heads + h]
    dsb = coef_ref[3 * n_heads + h]

    @pl.when(pl.program_id(1) == 0)
    def _tables():
        rows = lax.broadcasted_iota(jnp.int32, (c, c), 0).astype(F32)
        cols = lax.broadcasted_iota(jnp.int32, (c, c), 1).astype(F32)
        diff = rows - cols
        m_ref[...] = (jnp.where(diff >= 0, jnp.exp(lgf * jnp.maximum(diff, 0.0)), 0.0) * dsf
                      + jnp.where(diff <= 0, jnp.exp(lgb * jnp.maximum(-diff, 0.0)), 0.0) * dsb)
        xf_ref[...] = dsf * jnp.exp(lgf * (rows + 1.0))
        xb_ref[...] = dsb * jnp.exp(lgb * (c - rows))
        zf_ref[...] = jnp.exp(lgf * (c - 1.0 - rows))
        zb_ref[...] = jnp.exp(lgb * rows)

    chunk_len = jnp.full((1, c), float(c), F32)
    gcf = jnp.exp(lgf * chunk_len)
    gcb = jnp.exp(lgb * chunk_len)

    k_scale = HEAD_DIM_C ** -0.5
    if not latent and n_chunks == 1:
        for si in range(q_ref.shape[0] // c):
            rs = slice(si * c, (si + 1) * c)
            k = k_ref[rs, :] * k_scale
            vb = v_ref[rs, :].astype(BF16)
            a = _dot_nt(q_ref[rs, :].astype(BF16), k.astype(BF16))
            o = _dot((a * m_ref[...]).astype(BF16), vb)
            so_ref[si, layer_slot, 0] = _dot((k * zf_ref[...]).T.astype(BF16), vb)
            so_ref[si, layer_slot, 1] = _dot((k * zb_ref[...]).T.astype(BF16), vb)
            o_ref[rs, :] = (_rms(o, gn_ref[...]) * _silu(g_ref[rs, :])).astype(BF16)
        return

    if latent:
        sf_ref[...] = s0_ref[0]
        sb_ref[...] = s0_ref[1]
    else:
        sf_ref[...] = jnp.zeros((c, c), F32)
        sb_ref[...] = jnp.zeros((c, c), F32)
    use_cross = latent or n_chunks > 1

    def forward(ci, carry):
        sl = pl.ds(pl.multiple_of(ci * c, c), c)
        q = q_ref[sl, :]
        k = k_ref[sl, :] * k_scale
        if latent:
            cos = cos_ref[sl, :]
            sin = sin_ref[sl, :]
            q = q * cos + _swap_halves_128(q) * sin
            k = k * cos + _swap_halves_128(k) * sin
        qb = q.astype(BF16)
        vb = v_ref[sl, :].astype(BF16)
        qs_ref[sl, :] = qb
        ks_ref[sl, :] = k
        a = _dot_nt(qb, k.astype(BF16))
        o = _dot((a * m_ref[...]).astype(BF16), vb)
        if use_cross:
            o = o + _dot(qb, sf_ref[...].astype(BF16)) * xf_ref[...]
        oacc_ref[sl, :] = o
        sf_ref[...] = gcf * sf_ref[...] + _dot((k * zf_ref[...]).T.astype(BF16), vb)
        return carry

    def backward(cj, carry):
        ci = n_chunks - 1 - cj
        sl = pl.ds(pl.multiple_of(ci * c, c), c)
        k = ks_ref[sl, :]
        vb = v_ref[sl, :].astype(BF16)
        o = oacc_ref[sl, :]
        if use_cross:
            o = o + _dot(qs_ref[sl, :], sb_ref[...].astype(BF16)) * xb_ref[...]
        sb_ref[...] = gcb * sb_ref[...] + _dot((k * zb_ref[...]).T.astype(BF16), vb)
        o_ref[sl, :] = (_rms(o, gn_ref[...]) * _silu(g_ref[sl, :])).astype(BF16)
        return carry

    lax.fori_loop(0, n_chunks, forward, 0)
    lax.fori_loop(0, n_chunks, backward, 0)
    if not latent:
        so_ref[0, layer_slot, 0] = sf_ref[...]
        so_ref[0, layer_slot, 1] = sb_ref[...]


def _retention(z, coef, gn_g, batch, seq, n_heads, col0, layer, depth=None, cos=None, sin=None, state=None,
               prev_states=None):
    t = z.shape[0]
    dh = HEAD_DIM_C
    latent = state is not None
    n_chunks = seq // RET_CHUNK
    cb = col0 // dh
    seqs = 4 if (not latent and n_chunks == 1 and batch % 4 == 0) else 1
    rows = seqs * seq
    zspec = lambda seg: pl.BlockSpec((rows, dh), lambda h, b: (b, cb + seg * n_heads + h))
    in_specs = [
        pl.BlockSpec(memory_space=pltpu.SMEM),
        zspec(0), zspec(1), zspec(2), zspec(3),
        pl.BlockSpec((1, dh), lambda h, b: (0, h)),
    ]
    args = [coef, z, z, z, z, gn_g.reshape(1, -1)]
    ospec = pl.BlockSpec((rows, dh), lambda h, b: (b, h))
    oshape = jax.ShapeDtypeStruct((t, n_heads * dh), BF16)
    aliases = {}
    layer_slot = 0
    if latent:
        tab = pl.BlockSpec((seq, dh), lambda h, b: (0, 0))
        state_spec = pl.BlockSpec((None, None, 2, None, dh, dh), lambda h, b: (b, layer, 0, h, 0, 0))
        in_specs += [tab, tab, state_spec]
        args += [cos, sin, state]
        out_specs, out_shape = ospec, oshape
    else:
        if prev_states is None:
            layer_slot = layer
            state_spec = pl.BlockSpec((seqs, depth, 2, None, dh, dh), lambda h, b: (b, 0, 0, h, 0, 0))
        else:
            state_spec = pl.BlockSpec((seqs, 1, 2, None, dh, dh), lambda h, b: (b, layer, 0, h, 0, 0))
            aliases = {len(args): 1}
            args.append(prev_states)
            in_specs.append(pl.BlockSpec(memory_space=pl.ANY))
        out_specs = [ospec, state_spec]
        out_shape = [oshape, jax.ShapeDtypeStruct((batch, depth, 2, n_heads, dh, dh), F32)]
    sq = lambda: pltpu.VMEM((RET_CHUNK, RET_CHUNK), F32)
    kern = functools.partial(_ret_kernel, n_chunks=n_chunks, n_heads=n_heads, latent=latent, layer_slot=layer_slot)
    return pl.pallas_call(
        kern,
        grid=(n_heads, batch // seqs),
        in_specs=in_specs,
        out_specs=out_specs,
        out_shape=out_shape,
        input_output_aliases=aliases,
        scratch_shapes=[sq(), sq(), sq(), sq(), sq(), sq(), sq(),
                        pltpu.VMEM((seq, dh), F32), pltpu.VMEM((seq, dh), BF16), pltpu.VMEM((seq, dh), F32)],
        compiler_params=_cparams(2, 48),
        name="retention",
    )(*args)


def _out_proj_kernel(oa_ref, ob_ref, oc_ref, w_ref, x_ref, m_ref, y_ref, *, row0, tiles_per_cond):
    wa = oa_ref.shape[1]
    wb = ob_ref.shape[1]
    row = row0 + pl.program_id(0) // tiles_per_cond
    acc = _dot(oa_ref[...], w_ref[0:wa, :])
    acc = acc + _dot(ob_ref[...], w_ref[wa:wa + wb, :])
    acc = acc + _dot(oc_ref[...], w_ref[wa + wb:, :])
    y_ref[...] = x_ref[...] + m_ref[pl.ds(row, 1), :] * acc


def _out_proj(oa, ob, oc, w, x, mod, layer, row0, tokens_per_cond):
    t, d = x.shape
    tm, tn = 1024, 1024
    gate_col0 = (2 * d) // tn
    kern = functools.partial(_out_proj_kernel, row0=row0, tiles_per_cond=tokens_per_cond // tm)
    act = lambda a: pl.BlockSpec((tm, a.shape[1]), lambda i, j: (i, 0))
    return pl.pallas_call(
        kern,
        grid=(t // tm, d // tn),
        in_specs=[
            act(oa), act(ob), act(oc),
            pl.BlockSpec((None, d, tn), lambda i, j: (layer, 0, j)),
            pl.BlockSpec((tm, tn), lambda i, j: (i, j)),
            pl.BlockSpec((None, COND_ROWS, tn), lambda i, j: (layer, 0, gate_col0 + j)),
        ],
        out_specs=pl.BlockSpec((tm, tn), lambda i, j: (i, j)),
        out_shape=jax.ShapeDtypeStruct((t, d), F32),
        compiler_params=_cparams(2, 56),
        name="out_proj",
    )(oa, ob, oc, w, x, mod)


def _rope_angles(pos, n_dims):
    half = n_dims // 2
    freqs = ROPE_BASE ** (-jnp.arange(half, dtype=F32) / half)
    return pos.astype(F32)[:, None] * freqs[None, :]


def _axial_tables(seq):
    t = jnp.arange(seq)
    ar = _rope_angles(t // GRID_W, HEAD_DIM_A // 2)
    ac = _rope_angles(t % GRID_W, HEAD_DIM_A // 2)
    cos = jnp.concatenate([jnp.cos(ar), jnp.cos(ar), jnp.cos(ac), jnp.cos(ac)], -1)
    sin = jnp.concatenate([-jnp.sin(ar), jnp.sin(ar), -jnp.sin(ac), jnp.sin(ac)], -1)
    return cos, sin


def _ret_tables(seq):
    ang = _rope_angles(jnp.arange(seq), HEAD_DIM_C)
    cos = jnp.concatenate([jnp.cos(ang), jnp.cos(ang)], -1)
    sin = jnp.concatenate([-jnp.sin(ang), jnp.sin(ang)], -1)
    return cos, sin


def kernel(x_prompt, x_sample, cache_k, cache_v, state_ret, c, c_ctx, norm_g, w_mod, b_mod, w_in, qn_g, kn_g,
           rpb, conv_w, conv_b, cln_g, cln_b, w_pw, ret_decay_logit, ret_dir_scale, ret_gn_g, w_out):
    batch, seq, d = x_prompt.shape
    dec_batch, dec_seq, _ = x_sample.shape
    depth = w_in.shape[0]
    n_heads_a = cache_k.shape[3]
    w_a = n_heads_a * HEAD_DIM_A
    w_b = w_pw.shape[1]
    n_heads_c = state_ret.shape[3]
    w_c = n_heads_c * HEAD_DIM_C
    col_b = w_a
    col_c = col_b + 3 * w_b
    assert w_in.shape[2] == 3 * w_a + col_c + 4 * w_c and w_a + w_b + w_c == d
    assert 1 + dec_batch <= COND_ROWS
    t_ctx = batch * seq
    t_lat = dec_batch * dec_seq

    cond = jnp.concatenate([c_ctx[None], c, jnp.zeros((COND_ROWS - 1 - dec_batch, d), F32)], axis=0)
    mod = _modulation(cond, w_mod, b_mod)
    w_in_b = w_in.astype(BF16)
    w_out_b = w_out.astype(BF16)
    w_pw_b = w_pw.astype(BF16)
    ax_cos, ax_sin = _axial_tables(dec_seq)
    rt_cos, rt_sin = _ret_tables(dec_seq)
    log_gamma = jax.nn.log_sigmoid(ret_decay_logit.astype(F32))
    ret_coef = jnp.concatenate([log_gamma, ret_dir_scale.astype(F32)], axis=1).reshape(depth, -1)

    xp = x_prompt.reshape(t_ctx, d)
    xs = x_sample.reshape(t_lat, d)
    caches = None
    states = None
    for l in range(depth):
        conv_args = (conv_w[l], conv_b[l], cln_g[l], cln_b[l], w_pw_b, l)
        h = _norm_mod(xp, norm_g[l], mod, l, 0, t_ctx)
        q, *caches = _qkv_proj(h, w_in_b, l, qn_g[l], kn_g[l], w_a, seq, batch=batch, depth=depth, caches=caches)
        z = _in_proj(h, w_in_b, l, 3 * w_a)
        oa = _ctx_attention(q, *caches, l, z, batch, seq, w_a)
        ob = _conformer(z, *conv_args, seq, w_b, col_b)
        oc, states = _retention(z, ret_coef[l], ret_gn_g[l], batch, seq, n_heads_c, col_c, l, depth=depth,
                                prev_states=states)
        xp = _out_proj(oa, ob, oc, w_out_b, xp, mod, l, 0, t_ctx)
        h = _norm_mod(xs, norm_g[l], mod, l, 1, dec_seq)
        qr, kr, vb = _qkv_proj(h, w_in_b, l, qn_g[l], kn_g[l], w_a, dec_seq, cos=ax_cos, sin=ax_sin)
        z = _in_proj(h, w_in_b, l, 3 * w_a)
        oa = _nbr_attention(qr, kr, vb, cache_k, cache_v, l, z, rpb[l], dec_batch, dec_seq, w_a)
        ob = _conformer(z, *conv_args, dec_seq, w_b, col_b)
        oc = _retention(z, ret_coef[l], ret_gn_g[l], dec_batch, dec_seq, n_heads_c, col_c, l,
                        cos=rt_cos, sin=rt_sin, state=state_ret)
        xs = _out_proj(oa, ob, oc, w_out_b, xs, mod, l, 1, dec_seq)
    new_k, new_v = (a.reshape(batch, depth, seq, n_heads_a, HEAD_DIM_A) for a in caches)
    return (xp.reshape(batch, seq, d), xs.reshape(dec_batch, dec_seq, d), new_k, new_v, states)
```
